```python
import math
import jax
import jax.numpy as jnp
from jax import lax

D_MODEL = 1024
BATCH = 2
SEQ = 16384
DEPTH = 4
DEC_BATCH = 8
DEC_SEQ = 16
PAST_LEN = 2048

CHUNK = 64
MIX_WIDTH = D_MODEL
CONV_W = 4
LRU_WIDTH = MIX_WIDTH // 4
LRU_BLOCKS = 4
LRU_BLOCK = LRU_WIDTH // LRU_BLOCKS
LRU_C = 8.0
SSD_WIDTH = MIX_WIDTH // 4
SSD_HEAD_DIM = 64
SSD_HEADS = SSD_WIDTH // SSD_HEAD_DIM
SSD_GROUPS = 2
SSD_STATE = 128
SSD_CONV_CH = SSD_WIDTH + 2 * SSD_GROUPS * SSD_STATE
SSD_CHUNK = CHUNK
ATT_WIDTH = MIX_WIDTH - LRU_WIDTH - SSD_WIDTH
ATT_HEADS = 4
ATT_V_DIM = ATT_WIDTH // ATT_HEADS
ATT_HEAD_DIM = ATT_V_DIM // 2
ATT_QK = ATT_HEADS * 2 * ATT_HEAD_DIM
ATT_SCALE = ATT_HEAD_DIM ** -0.5
Q_BLOCK = 128
D_FF = 4 * D_MODEL
EPS = 1e-6
IN_SEGMENTS = (LRU_WIDTH, LRU_WIDTH, SSD_WIDTH, SSD_CONV_CH, SSD_HEADS, ATT_QK, ATT_QK, ATT_WIDTH)
IN_COLS = sum(IN_SEGMENTS)
IN_SPLITS = tuple(sum(IN_SEGMENTS[:i + 1]) for i in range(len(IN_SEGMENTS) - 1))

kernel_name = "hymba_style_lru_ssd_diffattn_stream_step"


def rms_norm(x, g):
    xf = x.astype(jnp.float32)
    y = xf * lax.rsqrt(jnp.mean(xf * xf, axis=-1, keepdims=True) + EPS)
    return (y * g.astype(jnp.float32)).astype(x.dtype)


def causal_conv(x, prev, w, b):
    L = x.shape[1]
    xp = jnp.concatenate([prev.astype(x.dtype), x], axis=1)
    y = b.astype(x.dtype) + w[0] * xp[:, 0:L]
    for j in range(1, CONV_W):
        y = y + w[j] * xp[:, j:j + L]
    return y, xp[:, L:]


def rg_lru(x, h0, wa, ba, wx, bx, lam):
    bsz, L, W = x.shape
    xb = x.reshape(bsz, L, LRU_BLOCKS, LRU_BLOCK)
    r = jax.nn.sigmoid((jnp.einsum("blhi,hij->blhj", xb, wa).reshape(bsz, L, W) + ba).astype(jnp.float32))
    i = jax.nn.sigmoid((jnp.einsum("blhi,hij->blhj", xb, wx).reshape(bsz, L, W) + bx).astype(jnp.float32))
    log_a = -LRU_C * r * jax.nn.softplus(-lam.astype(jnp.float32))
    a = jnp.exp(log_a)
    b = jnp.sqrt(-jnp.expm1(2.0 * log_a)) * (i * x.astype(jnp.float32))
    b = b.at[:, 0].add(a[:, 0] * h0.astype(jnp.float32))

    def combine(left, right):
        a1, b1 = left
        a2, b2 = right
        return a1 * a2, a2 * b1 + b2

    _, h = lax.associative_scan(combine, (a, b), axis=1)
    return h.astype(x.dtype), h[:, -1].astype(x.dtype)


def ssd_scan(xs, dt, a_neg, bm, cm, h0):
    bsz, L, H, P = xs.shape
    T = min(SSD_CHUNK, L)
    nc = L // T
    rep = H // SSD_GROUPS
    bh = jnp.repeat(bm, rep, axis=2).reshape(bsz, nc, T, H, SSD_STATE).astype(jnp.float32)
    ch = jnp.repeat(cm, rep, axis=2).reshape(bsz, nc, T, H, SSD_STATE).astype(jnp.float32)
    x_c = xs.reshape(bsz, nc, T, H, P).astype(jnp.float32)
    dt_c = dt.reshape(bsz, nc, T, H)
    cs = jnp.cumsum(dt_c * a_neg, axis=2)
    seg = cs[:, :, :, None, :] - cs[:, :, None, :, :]
    causal = jnp.tril(jnp.ones((T, T), dtype=bool))[None, None, :, :, None]
    decay = jnp.exp(jnp.where(causal, seg, -jnp.inf))
    xdt = x_c * dt_c[..., None]
    scores = jnp.einsum("bcihn,bcjhn->bcijh", ch, bh) * decay
    y_intra = jnp.einsum("bcijh,bcjhp->bcihp", scores, xdt)
    tail = jnp.exp(cs[:, :, -1:, :] - cs)
    s_local = jnp.einsum("bcjh,bcjhn,bcjhp->bchpn", tail, bh, xdt)
    chunk_decay = jnp.exp(cs[:, :, -1, :])

    def step(h, inp):
        dec, s = inp
        return dec[..., None, None] * h + s, h

    h_last, h_starts = lax.scan(step, h0.astype(jnp.float32),
                                (jnp.moveaxis(chunk_decay, 1, 0), jnp.moveaxis(s_local, 1, 0)))
    h_starts = jnp.moveaxis(h_starts, 0, 1)
    y_inter = jnp.einsum("bcihn,bchpn,bcih->bcihp", ch, h_starts, jnp.exp(cs))
    y = (y_intra + y_inter).reshape(bsz, L, H, P)
    return y, h_last


def diff_attn_block(q, qpos, k, v, kpos, lam):
    s = jnp.einsum("bqhmd,bkhmd->bhmqk", q, k, preferred_element_type=jnp.float32) * ATT_SCALE
    mask = (kpos[None, :] // CHUNK) <= (qpos[:, None] // CHUNK)
    s = jnp.where(mask, s, -jnp.inf)
    p = jax.nn.softmax(s, axis=-1)
    w = p[:, :, 0] - lam * p[:, :, 1]
    return jnp.einsum("bhqk,bkhe->bqhe", w.astype(v.dtype), v)


def diff_attention(q, qpos, k, v, kpos, lam):
    bsz, L = q.shape[:2]
    if L <= Q_BLOCK:
        return diff_attn_block(q, qpos, k, v, kpos, lam)
    nb = L // Q_BLOCK
    qb = jnp.moveaxis(q.reshape(bsz, nb, Q_BLOCK, *q.shape[2:]), 1, 0)
    pb = qpos.reshape(nb, Q_BLOCK)
    ob = lax.map(lambda a: diff_attn_block(a[0], a[1], k, v, kpos, lam), (qb, pb))
    return jnp.moveaxis(ob, 0, 1).reshape(bsz, L, *ob.shape[3:])


def layer_step(x, p, lam_init, conv_lru0, h_lru0, conv_ssd0, h_ssd0, past_k, past_v):
    bsz, L, _ = x.shape
    adt = x.dtype
    xn = rms_norm(x, p["norm_mix_g"])
    proj = jnp.einsum("bld,dc->blc", xn, p["w_in"])
    gate_r, x_r, z, xbc, dt_raw, q, k, v = jnp.split(proj, IN_SPLITS, axis=-1)

    x_r, conv_lru = causal_conv(x_r, conv_lru0, p["lru_conv_w"], p["lru_conv_b"])
    h_seq, h_lru = rg_lru(x_r, h_lru0, p["lru_wa"], p["lru_ba"], p["lru_wx"], p["lru_bx"], p["lru_lambda"])
    y_lru = jax.nn.gelu(gate_r) * h_seq

    xbc_c, conv_ssd = causal_conv(xbc, conv_ssd0, p["ssd_conv_w"], p["ssd_conv_b"])
    xbc_c = jax.nn.silu(xbc_c)
    xs, bm, cm = jnp.split(xbc_c, (SSD_WIDTH, SSD_WIDTH + SSD_GROUPS * SSD_STATE), axis=-1)
    xs = xs.reshape(bsz, L, SSD_HEADS, SSD_HEAD_DIM)
    bm = bm.reshape(bsz, L, SSD_GROUPS, SSD_STATE)
    cm = cm.reshape(bsz, L, SSD_GROUPS, SSD_STATE)
    dt = jax.nn.softplus(dt_raw.astype(jnp.float32) + p["ssd_dt_bias"].astype(jnp.float32))
    a_neg = -jnp.exp(p["ssd_a_log"].astype(jnp.float32))
    y_ssd, h_ssd = ssd_scan(xs, dt, a_neg, bm, cm, h_ssd0)
    y_ssd = y_ssd + p["ssd_d"].astype(jnp.float32)[:, None] * xs.astype(jnp.float32)
    y_ssd = y_ssd.reshape(bsz, L, SSD_WIDTH) * jax.nn.silu(z.astype(jnp.float32))
    y_ssd = rms_norm(y_ssd.reshape(bsz, L, SSD_GROUPS, SSD_WIDTH // SSD_GROUPS),
                     p["ssd_norm_g"].reshape(SSD_GROUPS, SSD_WIDTH // SSD_GROUPS))
    y_ssd = y_ssd.reshape(bsz, L, SSD_WIDTH).astype(adt)

    q = q.reshape(bsz, L, ATT_HEADS, 2, ATT_HEAD_DIM)
    k_rows = k.reshape(bsz, L, ATT_HEADS, ATT_V_DIM)
    v_rows = v.reshape(bsz, L, ATT_HEADS, ATT_V_DIM)
    if past_k is None:
        past_len = 0
        k_all, v_all = k_rows, v_rows
    else:
        past_len = past_k.shape[1]
        k_all = jnp.concatenate([past_k.astype(adt), k_rows], axis=1)
        v_all = jnp.concatenate([past_v.astype(adt), v_rows], axis=1)
    n_keys = k_all.shape[1]
    qpos = past_len + jnp.arange(L)
    kpos = jnp.arange(n_keys)
    lv = p["att_lambda"].astype(jnp.float32)
    lam = jnp.exp(jnp.sum(lv[0] * lv[1])) - jnp.exp(jnp.sum(lv[2] * lv[3])) + lam_init
    o = diff_attention(q, qpos, k_all.reshape(bsz, n_keys, ATT_HEADS, 2, ATT_HEAD_DIM), v_all, kpos, lam)
    o = rms_norm(o, p["att_subln_g"]) * (1.0 - lam_init)
    y_att = o.reshape(bsz, L, ATT_WIDTH).astype(adt)

    mixed = jnp.concatenate([y_lru, y_ssd, y_att], axis=-1)
    x = x + jnp.einsum("blc,cd->bld", mixed, p["w_out"])

    hn = rms_norm(x, p["norm_ffn_g"])
    u = jnp.square(jax.nn.relu(jnp.einsum("bld,df->blf", hn, p["w_up"])))
    x = x + jnp.einsum("blf,fd->bld", u, p["w_down"])
    return x, (k_rows, v_rows, conv_lru, h_lru, conv_ssd, h_ssd)


def setup_inputs(seed: int = 0) -> dict:
    key = jax.random.key(seed)
    ks = iter(jax.random.split(key, 40))

    def nrm(shape, scale):
        return scale * jax.random.normal(next(ks), shape, jnp.float32)

    def uni(shape, lo, hi):
        return jax.random.uniform(next(ks), shape, jnp.float32, minval=lo, maxval=hi)

    x_prompt = nrm((BATCH, SEQ, D_MODEL), 1.0)
    x_sample = nrm((DEC_BATCH, DEC_SEQ, D_MODEL), 1.0)
    cache_att_k = nrm((DEPTH, DEC_BATCH, PAST_LEN, ATT_HEADS, ATT_V_DIM), 1.0)
    cache_att_v = nrm((DEPTH, DEC_BATCH, PAST_LEN, ATT_HEADS, ATT_V_DIM), 1.0)
    state_lru_conv = nrm((DEPTH, DEC_BATCH, CONV_W - 1, LRU_WIDTH), 1.0)
    state_lru_h = nrm((DEPTH, DEC_BATCH, LRU_WIDTH), 0.5)
    state_ssd_conv = nrm((DEPTH, DEC_BATCH, CONV_W - 1, SSD_CONV_CH), 1.0)
    state_ssd_h = nrm((DEPTH, DEC_BATCH, SSD_HEADS, SSD_HEAD_DIM, SSD_STATE), 0.1)

    norm_mix_g = 1.0 + nrm((DEPTH, D_MODEL), 0.02)
    w_in = nrm((DEPTH, D_MODEL, IN_COLS), D_MODEL ** -0.5)
    lru_conv_w = nrm((DEPTH, CONV_W, LRU_WIDTH), CONV_W ** -0.5)
    lru_conv_b = nrm((DEPTH, LRU_WIDTH), 0.02)
    lru_wa = nrm((DEPTH, LRU_BLOCKS, LRU_BLOCK, LRU_BLOCK), LRU_BLOCK ** -0.5)
    lru_ba = nrm((DEPTH, LRU_WIDTH), 0.02)
    lru_wx = nrm((DEPTH, LRU_BLOCKS, LRU_BLOCK, LRU_BLOCK), LRU_BLOCK ** -0.5)
    lru_bx = nrm((DEPTH, LRU_WIDTH), 0.02)
    a8 = uni((DEPTH, LRU_WIDTH), 0.9, 0.999)
    s_lam = a8 ** (1.0 / LRU_C)
    lru_lambda = jnp.log(s_lam) - jnp.log1p(-s_lam)
    ssd_conv_w = nrm((DEPTH, CONV_W, SSD_CONV_CH), CONV_W ** -0.5)
    ssd_conv_b = nrm((DEPTH, SSD_CONV_CH), 0.02)
    dt0 = jnp.exp(uni((DEPTH, SSD_HEADS), math.log(1e-3), math.log(1e-1)))
    ssd_dt_bias = dt0 + jnp.log(-jnp.expm1(-dt0))
    ssd_a_log = jnp.log(uni((DEPTH, SSD_HEADS), 1.0, 16.0))
    ssd_d = 1.0 + nrm((DEPTH, SSD_HEADS), 0.1)
    ssd_norm_g = 1.0 + nrm((DEPTH, SSD_WIDTH), 0.02)
    att_lambda = nrm((DEPTH, 4, ATT_HEAD_DIM), 0.1)
    att_subln_g = 1.0 + nrm((DEPTH, ATT_V_DIM), 0.02)
    w_out = nrm((DEPTH, MIX_WIDTH, D_MODEL), MIX_WIDTH ** -0.5)
    norm_ffn_g = 1.0 + nrm((DEPTH, D_MODEL), 0.02)
    w_up = nrm((DEPTH, D_MODEL, D_FF), D_MODEL ** -0.5)
    w_down = nrm((DEPTH, D_FF, D_MODEL), D_FF ** -0.5)
    norm_f_g = 1.0 + nrm((D_MODEL,), 0.02)
    return {
        "x_prompt": x_prompt, "x_sample": x_sample,
        "cache_att_k": cache_att_k, "cache_att_v": cache_att_v,
        "state_lru_conv": state_lru_conv, "state_lru_h": state_lru_h,
        "state_ssd_conv": state_ssd_conv, "state_ssd_h": state_ssd_h,
        "norm_mix_g": norm_mix_g, "w_in": w_in,
        "lru_conv_w": lru_conv_w, "lru_conv_b": lru_conv_b,
        "lru_wa": lru_wa, "lru_ba": lru_ba, "lru_wx": lru_wx, "lru_bx": lru_bx,
        "lru_lambda": lru_lambda,
        "ssd_conv_w": ssd_conv_w, "ssd_conv_b": ssd_conv_b, "ssd_dt_bias": ssd_dt_bias,
        "ssd_a_log": ssd_a_log, "ssd_d": ssd_d, "ssd_norm_g": ssd_norm_g,
        "att_lambda": att_lambda, "att_subln_g": att_subln_g,
        "w_out": w_out, "norm_ffn_g": norm_ffn_g, "w_up": w_up, "w_down": w_down,
        "norm_f_g": norm_f_g,
    }


def reference(x_prompt, x_sample, cache_att_k, cache_att_v, state_lru_conv, state_lru_h,
              state_ssd_conv, state_ssd_h, norm_mix_g, w_in, lru_conv_w, lru_conv_b,
              lru_wa, lru_ba, lru_wx, lru_bx, lru_lambda, ssd_conv_w, ssd_conv_b,
              ssd_dt_bias, ssd_a_log, ssd_d, ssd_norm_g, att_lambda, att_subln_g,
              w_out, norm_ffn_g, w_up, w_down, norm_f_g):
    xp = x_prompt
    xs = x_sample
    bp = xp.shape[0]
    zero_lru_conv = jnp.zeros((bp, CONV_W - 1, LRU_WIDTH), xp.dtype)
    zero_lru_h = jnp.zeros((bp, LRU_WIDTH), xp.dtype)
    zero_ssd_conv = jnp.zeros((bp, CONV_W - 1, SSD_CONV_CH), xp.dtype)
    zero_ssd_h = jnp.zeros((bp, SSD_HEADS, SSD_HEAD_DIM, SSD_STATE), xp.dtype)
    st_p_all = []
    st_s_all = []
    for l in range(DEPTH):
        p = {
            "norm_mix_g": norm_mix_g[l], "w_in": w_in[l],
            "lru_conv_w": lru_conv_w[l], "lru_conv_b": lru_conv_b[l],
            "lru_wa": lru_wa[l], "lru_ba": lru_ba[l], "lru_wx": lru_wx[l], "lru_bx": lru_bx[l],
            "lru_lambda": lru_lambda[l],
            "ssd_conv_w": ssd_conv_w[l], "ssd_conv_b": ssd_conv_b[l],
            "ssd_dt_bias": ssd_dt_bias[l], "ssd_a_log": ssd_a_log[l], "ssd_d": ssd_d[l],
            "ssd_norm_g": ssd_norm_g[l],
            "att_lambda": att_lambda[l], "att_subln_g": att_subln_g[l],
            "w_out": w_out[l], "norm_ffn_g": norm_ffn_g[l], "w_up": w_up[l], "w_down": w_down[l],
        }
        lam_init = 0.8 - 0.6 * math.exp(-0.3 * l)
        xp, st_p = layer_step(xp, p, lam_init, zero_lru_conv, zero_lru_h, zero_ssd_conv, zero_ssd_h,
                              None, None)
        xs, st_s = layer_step(xs, p, lam_init, state_lru_conv[l], state_lru_h[l], state_ssd_conv[l],
                              state_ssd_h[l], cache_att_k[l], cache_att_v[l])
        st_p_all.append(st_p)
        st_s_all.append(st_s)

    def stack(sts, i):
        return jnp.stack([s[i] for s in sts], axis=0)

    y_prompt = rms_norm(xp, norm_f_g)
    y_sample = rms_norm(xs, norm_f_g)
    new_k_p = stack(st_p_all, 0)
    new_v_p = stack(st_p_all, 1)
    lru_conv_p = stack(st_p_all, 2)
    lru_h_p = stack(st_p_all, 3)
    ssd_conv_p = stack(st_p_all, 4)
    ssd_h_p = stack(st_p_all, 5)
    new_k_s = stack(st_s_all, 0)
    new_v_s = stack(st_s_all, 1)
    lru_conv_s = stack(st_s_all, 2)
    lru_h_s = stack(st_s_all, 3)
    ssd_conv_s = stack(st_s_all, 4)
    ssd_h_s = stack(st_s_all, 5)
    return (y_prompt, y_sample, new_k_p, new_v_p, lru_conv_p, lru_h_p, ssd_conv_p, ssd_h_p,
            new_k_s, new_v_s, lru_conv_s, lru_h_s, ssd_conv_s, ssd_h_s)
```

```python
import functools
import math

import jax
import jax.numpy as jnp
from jax import lax
from jax.experimental import pallas as pl
from jax.experimental.pallas import tpu as pltpu

D_MODEL = 1024
DEPTH = 4
CHUNK = 64
CONV_W = 4
LRU_WIDTH = 256
LRU_BLOCKS = 4
LRU_BLOCK = LRU_WIDTH // LRU_BLOCKS
LRU_C = 8.0
SSD_WIDTH = 256
SSD_HEAD_DIM = 64
SSD_HEADS = 4
SSD_GROUPS = 2
SSD_STATE = 128
SSD_CONV_CH = SSD_WIDTH + 2 * SSD_GROUPS * SSD_STATE
ATT_WIDTH = 512
ATT_HEADS = 4
ATT_V_DIM = 128
ATT_HEAD_DIM = 64
ATT_SCALE = ATT_HEAD_DIM ** -0.5
D_FF = 4 * D_MODEL
EPS = 1e-6

LANES = 128
CONV_PAD = 8
DT_PAD = LANES
LRU_IN_W = 2 * LRU_WIDTH
SSD_IN_W = SSD_WIDTH + SSD_CONV_CH + DT_PAD
IN_COLS_PADDED = LRU_IN_W + SSD_IN_W + 3 * ATT_WIDTH
NEG_BIG = -1e30
VMEM_LIMIT = 56 * 1024 * 1024

_ARB = "arbitrary"


def _params(*sem):
    return pltpu.CompilerParams(dimension_semantics=sem, vmem_limit_bytes=VMEM_LIMIT)


def _resident(shape):
    nd = len(shape)
    return pl.BlockSpec(shape, lambda *_: (0,) * nd, pipeline_mode=pl.Buffered(1))


def _rms(x, g):
    return x * lax.rsqrt(jnp.mean(x * x, axis=-1, keepdims=True) + EPS) * g


def _in_proj_kernel(x_ref, g_ref, w_ref, lru_ref, ssd_ref, q_ref, k_ref, v_ref, kb_ref, vb_ref):
    xn = _rms(x_ref[...], g_ref[...]).astype(jnp.bfloat16)

    def seg(lo, width):
        return jnp.dot(xn, w_ref[:, lo:lo + width], preferred_element_type=jnp.float32)

    lru_ref[...] = seg(0, LRU_IN_W)
    ssd_ref[...] = seg(LRU_IN_W, SSD_IN_W)
    lo = LRU_IN_W + SSD_IN_W
    q_ref[...] = (seg(lo, ATT_WIDTH) * ATT_SCALE).astype(jnp.bfloat16)
    k = seg(lo + ATT_WIDTH, ATT_WIDTH)
    k_ref[...] = k
    kb_ref[...] = k.astype(jnp.bfloat16)
    v = seg(lo + 2 * ATT_WIDTH, ATT_WIDTH)
    v_ref[...] = v
    vb_ref[...] = v.astype(jnp.bfloat16)


def _in_proj(x, g, w, tm):
    n = x.shape[0]
    row = lambda width: pl.BlockSpec((tm, width), lambda i: (i, 0))
    f32, bf16 = jnp.float32, jnp.bfloat16
    return pl.pallas_call(
        _in_proj_kernel,
        grid=(n // tm,),
        in_specs=[row(D_MODEL), _resident((1, D_MODEL)), _resident((D_MODEL, IN_COLS_PADDED))],
        out_specs=[row(LRU_IN_W), row(SSD_IN_W)] + [row(ATT_WIDTH)] * 5,
        out_shape=[jax.ShapeDtypeStruct((n, LRU_IN_W), f32), jax.ShapeDtypeStruct((n, SSD_IN_W), f32),
                   jax.ShapeDtypeStruct((n, ATT_WIDTH), bf16),
                   jax.ShapeDtypeStruct((n, ATT_WIDTH), f32), jax.ShapeDtypeStruct((n, ATT_WIDTH), f32),
                   jax.ShapeDtypeStruct((n, ATT_WIDTH), bf16), jax.ShapeDtypeStruct((n, ATT_WIDTH), bf16)],
        compiler_params=_params(_ARB),
        name="in_proj",
    )(x, g, w)


def _causal_conv(xp_ref, x, prev_ref, w_ref, b_ref, first):
    t = x.shape[0]
    keep = CONV_W - 1

    @pl.when(first)
    def _():
        xp_ref[CONV_PAD - keep:CONV_PAD, :] = prev_ref[0]

    @pl.when(jnp.logical_not(first))
    def _():
        xp_ref[CONV_PAD - keep:CONV_PAD, :] = xp_ref[CONV_PAD + t - keep:CONV_PAD + t, :]

    xp_ref[CONV_PAD:CONV_PAD + t, :] = x
    y = b_ref[...] + w_ref[CONV_W - 1:CONV_W, :] * x
    for j in range(CONV_W - 1):
        y = y + w_ref[j:j + 1, :] * xp_ref[CONV_PAD - keep + j:CONV_PAD - keep + j + t, :]
    return y


def _row_scan(step, vals, t):
    row = lax.broadcasted_iota(jnp.int32, (t, 1), 0)
    d = 1
    while d < t:
        shifted = tuple(pltpu.roll(v, d, 0) for v in vals)
        new = step(shifted, vals)
        vals = tuple(jnp.where(row >= d, n, v) for n, v in zip(new, vals))
        d *= 2
    return vals


def _lru_kernel(in_ref, conv0_ref, h0_ref, cw_ref, cb_ref, wg_ref, bg_ref, nsp_ref,
                y_ref, conv_out_ref, h_out_ref, xp_ref, h_ref):
    i = pl.program_id(1)
    first = i == 0
    t = in_ref.shape[0]

    @pl.when(first)
    def _():
        h_ref[...] = h0_ref[0]

    gate = in_ref[:, 0:LRU_WIDTH]
    x = _causal_conv(xp_ref, in_ref[:, LRU_WIDTH:LRU_IN_W], conv0_ref, cw_ref, cb_ref, first)
    pre = jnp.dot(x.astype(jnp.bfloat16), wg_ref[...], preferred_element_type=jnp.float32) + bg_ref[...]
    r = jax.nn.sigmoid(pre[:, 0:LRU_WIDTH])
    ig = jax.nn.sigmoid(pre[:, LRU_WIDTH:2 * LRU_WIDTH])
    log_a = r * nsp_ref[...]
    a = jnp.exp(log_a)
    b = jnp.sqrt(-jnp.tanh(log_a) * (a * a + 1.0)) * (ig * x)

    def step(shifted, cur):
        a_s, b_s = shifted
        a_c, b_c = cur
        return a_c * a_s, a_c * b_s + b_c

    a_cum, b_cum = _row_scan(step, (a, b), t)
    h = a_cum * h_ref[...] + b_cum
    h_ref[...] = h[t - 1:t, :]
    g = gate
    gelu = 0.5 * g * (1.0 + jnp.tanh(math.sqrt(2.0 / math.pi) * (g + 0.044715 * (g * g * g))))
    y_ref[...] = (gelu * h).astype(y_ref.dtype)

    @pl.when(i == pl.num_programs(1) - 1)
    def _():
        keep = CONV_W - 1
        conv_out_ref[0] = xp_ref[CONV_PAD + t - keep:CONV_PAD + t, :]
        h_out_ref[0] = h[t - 1:t, :]


def _lru(lru_in, conv0, h0, cw, cb, wg, bg, nsp, bsz, t):
    n = lru_in.shape[0]
    nt = n // bsz // t
    keep = CONV_W - 1
    per_b = lambda shape: pl.BlockSpec((1,) + shape, lambda b, i: (b,) + (0,) * len(shape))
    f32 = jnp.float32
    return pl.pallas_call(
        _lru_kernel,
        grid=(bsz, nt),
        in_specs=[pl.BlockSpec((t, LRU_IN_W), lambda b, i: (b * nt + i, 0)),
                  per_b((keep, LRU_WIDTH)), per_b((1, LRU_WIDTH)),
                  _resident((CONV_W, LRU_WIDTH)), _resident((1, LRU_WIDTH)),
                  _resident((LRU_WIDTH, 2 * LRU_WIDTH)), _resident((1, 2 * LRU_WIDTH)),
                  _resident((1, LRU_WIDTH))],
        out_specs=[pl.BlockSpec((t, LRU_WIDTH), lambda b, i: (b * nt + i, 0)),
                   per_b((keep, LRU_WIDTH)), per_b((1, LRU_WIDTH))],
        out_shape=[jax.ShapeDtypeStruct((n, LRU_WIDTH), jnp.bfloat16),
                   jax.ShapeDtypeStruct((bsz, keep, LRU_WIDTH), f32),
                   jax.ShapeDtypeStruct((bsz, 1, LRU_WIDTH), f32)],
        scratch_shapes=[pltpu.VMEM((CONV_PAD + t, LRU_WIDTH), f32), pltpu.VMEM((1, LRU_WIDTH), f32)],
        compiler_params=_params(_ARB, _ARB),
        name="lru",
    )(lru_in, conv0, h0, cw, cb, wg, bg, nsp)


def _ssd_kernel(in_ref, conv0_ref, h0_ref, cw_ref, cb_ref, dtb_ref, aneg_ref, dskip_ref, ng_ref,
                y_ref, conv_out_ref, h_out_ref, xp_ref, h_ref):
    i = pl.program_id(1)
    first = i == 0
    t = in_ref.shape[0]
    bf16 = jnp.bfloat16
    gw = 2 * SSD_HEAD_DIM

    @pl.when(first)
    def _():
        h_ref[...] = h0_ref[0]

    z = in_ref[:, 0:SSD_WIDTH]
    xc = _causal_conv(xp_ref, in_ref[:, SSD_WIDTH:SSD_WIDTH + SSD_CONV_CH], conv0_ref, cw_ref, cb_ref, first)
    xc = xc * jax.nn.sigmoid(xc)
    dt = jax.nn.softplus(in_ref[:, SSD_WIDTH + SSD_CONV_CH:SSD_IN_W] + dtb_ref[...])
    (cs,) = _row_scan(lambda s, c: (s[0] + c[0],), (dt * aneg_ref[...],), t)
    cs_t = cs.T
    cs_last = cs[t - 1:t, :]
    tail = jnp.exp(cs_last - cs)
    grow = jnp.exp(cs)
    chunk_decay = jnp.exp(cs_last)

    lane = lax.broadcasted_iota(jnp.int32, (t, gw), 1)
    low = lane < SSD_HEAD_DIM
    srow = lax.broadcasted_iota(jnp.int32, (gw, 1), 0) < SSD_HEAD_DIM
    causal = lax.broadcasted_iota(jnp.int32, (t, t), 0) >= lax.broadcasted_iota(jnp.int32, (t, t), 1)

    def pick(arr, g):
        c0 = jnp.broadcast_to(arr[:, 2 * g:2 * g + 1], (t, gw))
        c1 = jnp.broadcast_to(arr[:, 2 * g + 1:2 * g + 2], (t, gw))
        return jnp.where(low, c0, c1)

    ys = []
    for g in range(SSD_GROUPS):
        xs = xc[:, g * gw:(g + 1) * gw]
        bm = xc[:, SSD_WIDTH + g * SSD_STATE:SSD_WIDTH + (g + 1) * SSD_STATE].astype(bf16)
        cm = xc[:, SSD_WIDTH + (SSD_GROUPS + g) * SSD_STATE:SSD_WIDTH + (SSD_GROUPS + g + 1) * SSD_STATE].astype(bf16)
        xdt = xs * pick(dt, g)
        xdt_b = xdt.astype(bf16)
        cb = lax.dot_general(cm, bm, (((1,), (1,)), ((), ())), preferred_element_type=jnp.float32)
        y_heads = []
        for hh in range(2):
            hd = 2 * g + hh
            seg = jnp.broadcast_to(cs[:, hd:hd + 1], (t, t)) - cs_t[hd:hd + 1, :]
            scores = (cb * jnp.exp(jnp.where(causal, seg, NEG_BIG))).astype(bf16)
            y_heads.append(jnp.dot(scores, xdt_b, preferred_element_type=jnp.float32))
        y_intra = jnp.where(low, y_heads[0], y_heads[1])
        h_g = h_ref[g]
        y_inter = lax.dot_general(cm, h_g.astype(bf16), (((1,), (1,)), ((), ())),
                                  preferred_element_type=jnp.float32) * pick(grow, g)
        s_local = lax.dot_general((xdt * pick(tail, g)).astype(bf16), bm, (((0,), (0,)), ((), ())),
                                  preferred_element_type=jnp.float32)
        dec = jnp.where(srow, chunk_decay[:, 2 * g:2 * g + 1], chunk_decay[:, 2 * g + 1:2 * g + 2])
        h_ref[g] = dec * h_g + s_local
        y = y_intra + y_inter + dskip_ref[:, g * gw:(g + 1) * gw] * xs
        zg = z[:, g * gw:(g + 1) * gw]
        y = y * (zg * jax.nn.sigmoid(zg))
        ys.append(_rms(y, ng_ref[:, g * gw:(g + 1) * gw]))
    y_ref[...] = jnp.concatenate(ys, axis=1).astype(y_ref.dtype)

    @pl.when(i == pl.num_programs(1) - 1)
    def _():
        keep = CONV_W - 1
        conv_out_ref[0] = xp_ref[CONV_PAD + t - keep:CONV_PAD + t, :]
        h_out_ref[0] = h_ref[...]


def _ssd(ssd_in, conv0, h0, cw, cb, dtb, aneg, dskip, ng, bsz, t):
    n = ssd_in.shape[0]
    nt = n // bsz // t
    keep = CONV_W - 1
    gw = 2 * SSD_HEAD_DIM
    per_b = lambda shape: pl.BlockSpec((1,) + shape, lambda b, i: (b,) + (0,) * len(shape))
    f32 = jnp.float32
    return pl.pallas_call(
        _ssd_kernel,
        grid=(bsz, nt),
        in_specs=[pl.BlockSpec((t, SSD_IN_W), lambda b, i: (b * nt + i, 0)),
                  per_b((keep, SSD_CONV_CH)), per_b((SSD_GROUPS, gw, SSD_STATE)),
                  _resident((CONV_W, SSD_CONV_CH)), _resident((1, SSD_CONV_CH)),
                  _resident((1, DT_PAD)), _resident((1, DT_PAD)),
                  _resident((1, SSD_WIDTH)), _resident((1, SSD_WIDTH))],
        out_specs=[pl.BlockSpec((t, SSD_WIDTH), lambda b, i: (b * nt + i, 0)),
                   per_b((keep, SSD_CONV_CH)), per_b((SSD_GROUPS, gw, SSD_STATE))],
        out_shape=[jax.ShapeDtypeStruct((n, SSD_WIDTH), jnp.bfloat16),
                   jax.ShapeDtypeStruct((bsz, keep, SSD_CONV_CH), f32),
                   jax.ShapeDtypeStruct((bsz, SSD_GROUPS, gw, SSD_STATE), f32)],
        scratch_shapes=[pltpu.VMEM((CONV_PAD + t, SSD_CONV_CH), f32),
                        pltpu.VMEM((SSD_GROUPS, gw, SSD_STATE), f32)],
        compiler_params=_params(_ARB, _ARB),
        name="ssd",
    )(ssd_in, conv0, h0, cw, cb, dtb, aneg, dskip, ng)


def _stack_q(q):
    lane = lax.broadcasted_iota(jnp.int32, q.shape, 1)
    zero = jnp.zeros_like(q)
    return jnp.concatenate([jnp.where(lane < ATT_HEAD_DIM, q, zero),
                            jnp.where(lane < ATT_HEAD_DIM, zero, q)], axis=0)


def _diff_lambda(lv, lam_init):
    e1 = jnp.exp(jnp.sum(lv[0:1, :] * lv[1:2, :], axis=1, keepdims=True))
    e2 = jnp.exp(jnp.sum(lv[2:3, :] * lv[3:4, :], axis=1, keepdims=True))
    return e1 - e2 + lam_init


def _attn_finish(acc, l, lam, g, lam_init, bq):
    o = acc[0:bq] / l[0:bq] - lam * (acc[bq:2 * bq] / l[bq:2 * bq])
    return _rms(o, g) * (1.0 - lam_init)


def _attn_prompt_kernel(q_ref, k_ref, v_ref, lv_ref, g_ref, o_ref, qs_ref, m_ref, l_ref, acc_ref,
                        *, lam_init, bq):
    i = pl.program_id(2)
    rows = 2 * bq
    qs_ref[...] = _stack_q(q_ref[...])
    m_ref[...] = jnp.full((rows, 1), NEG_BIG, jnp.float32)
    l_ref[...] = jnp.zeros((rows, 1), jnp.float32)
    acc_ref[...] = jnp.zeros((rows, ATT_V_DIM), jnp.float32)

    def block(start, mask):
        k = k_ref[pl.ds(start, bq), :]
        v = v_ref[pl.ds(start, bq), :]
        s = lax.dot_general(qs_ref[...], k, (((1,), (1,)), ((), ())), preferred_element_type=jnp.float32)
        if mask is not None:
            s = jnp.where(mask, s, NEG_BIG)
        m_old = m_ref[...]
        m_new = jnp.maximum(m_old, jnp.max(s, axis=1, keepdims=True))
        p = jnp.exp(s - m_new)
        alpha = jnp.exp(m_old - m_new)
        l_ref[...] = alpha * l_ref[...] + jnp.sum(p, axis=1, keepdims=True)
        acc_ref[...] = alpha * acc_ref[...] + jnp.dot(p.astype(jnp.bfloat16), v,
                                                      preferred_element_type=jnp.float32)
        m_ref[...] = m_new

    def body(j, carry):
        block(pl.multiple_of(j * bq, bq), None)
        return carry

    lax.fori_loop(0, i, body, 0)
    qpos = lax.broadcasted_iota(jnp.int32, (rows, bq), 0) % bq
    kpos = lax.broadcasted_iota(jnp.int32, (rows, bq), 1)
    block(pl.multiple_of(i * bq, bq), (kpos // CHUNK) <= (qpos // CHUNK))
    lam = _diff_lambda(lv_ref[...], lam_init)
    o_ref[...] = _attn_finish(acc_ref[...], l_ref[...], lam, g_ref[...], lam_init, bq).astype(o_ref.dtype)


def _attn_prompt(q, kb, vb, lv, g, lam_init, bsz, bq):
    n = q.shape[0]
    seq = n // bsz
    nq = seq // bq
    assert bq % CHUNK == 0 and seq % bq == 0
    f32 = jnp.float32
    return pl.pallas_call(
        functools.partial(_attn_prompt_kernel, lam_init=lam_init, bq=bq),
        grid=(bsz, ATT_HEADS, nq),
        in_specs=[pl.BlockSpec((bq, ATT_V_DIM), lambda b, h, i: (b * nq + i, h)),
                  pl.BlockSpec((seq, ATT_V_DIM), lambda b, h, i: (b, h)),
                  pl.BlockSpec((seq, ATT_V_DIM), lambda b, h, i: (b, h)),
                  pl.BlockSpec((4, ATT_HEAD_DIM), lambda b, h, i: (0, 0)),
                  pl.BlockSpec((1, ATT_V_DIM), lambda b, h, i: (0, 0))],
        out_specs=pl.BlockSpec((bq, ATT_V_DIM), lambda b, h, i: (b * nq + i, h)),
        out_shape=jax.ShapeDtypeStruct((n, ATT_WIDTH), jnp.bfloat16),
        scratch_shapes=[pltpu.VMEM((2 * bq, ATT_V_DIM), jnp.bfloat16),
                        pltpu.VMEM((2 * bq, 1), f32), pltpu.VMEM((2 * bq, 1), f32),
                        pltpu.VMEM((2 * bq, ATT_V_DIM), f32)],
        compiler_params=_params(_ARB, _ARB, _ARB),
        name="attn_prompt",
    )(q, kb, vb, lv, g)


def _attn_sample_kernel(q_ref, kn_ref, vn_ref, kp_ref, vp_ref, lv_ref, g_ref, o_ref, *, lam_init, lq):
    lam = _diff_lambda(lv_ref[...], lam_init)
    bf16 = jnp.bfloat16
    nt = (((1,), (1,)), ((), ()))
    outs = []
    for h in range(ATT_HEADS):
        sl = slice(h * ATT_V_DIM, (h + 1) * ATT_V_DIM)
        qs = _stack_q(q_ref[:, sl])
        kp = kp_ref[:, sl].astype(bf16)
        vp = vp_ref[:, sl].astype(bf16)
        s_p = lax.dot_general(qs, kp, nt, preferred_element_type=jnp.float32)
        s_n = lax.dot_general(qs, kn_ref[:, sl], nt, preferred_element_type=jnp.float32)
        m = jnp.maximum(jnp.max(s_p, axis=1, keepdims=True), jnp.max(s_n, axis=1, keepdims=True))
        p_p = jnp.exp(s_p - m)
        p_n = jnp.exp(s_n - m)
        l = jnp.sum(p_p, axis=1, keepdims=True) + jnp.sum(p_n, axis=1, keepdims=True)
        acc = (jnp.dot(p_p.astype(bf16), vp, preferred_element_type=jnp.float32)
               + jnp.dot(p_n.astype(bf16), vn_ref[:, sl], preferred_element_type=jnp.float32))
        outs.append(_attn_finish(acc, l, lam, g_ref[...], lam_init, lq))
    o_ref[...] = jnp.concatenate(outs, axis=1).astype(o_ref.dtype)


def _attn_sample(q, kb, vb, past_k, past_v, layer, lv, g, lam_init, bsz, lq, past_len):
    assert past_len % CHUNK == 0 and lq <= CHUNK
    n = q.shape[0]
    step = lambda: pl.BlockSpec((lq, ATT_WIDTH), lambda b: (b, 0))
    past = lambda: pl.BlockSpec((past_len, ATT_WIDTH), lambda b: (layer * bsz + b, 0))
    return pl.pallas_call(
        functools.partial(_attn_sample_kernel, lam_init=lam_init, lq=lq),
        grid=(bsz,),
        in_specs=[step(), step(), step(), past(), past(),
                  pl.BlockSpec((4, ATT_HEAD_DIM), lambda b: (0, 0)),
                  pl.BlockSpec((1, ATT_V_DIM), lambda b: (0, 0))],
        out_specs=step(),
        out_shape=jax.ShapeDtypeStruct((n, ATT_WIDTH), jnp.bfloat16),
        compiler_params=_params(_ARB),
        name="attn_sample",
    )(q, kb, vb, past_k, past_v, lv, g)


def _out_ffn_kernel(x_ref, yl_ref, ys_ref, ya_ref, wo_ref, gf_ref, wu_ref, wd_ref, gn_ref, o_ref,
                    *, ff_chunk, final_norm):
    f32 = jnp.float32
    mixed = jnp.concatenate([yl_ref[...], ys_ref[...], ya_ref[...]], axis=1)
    x = x_ref[...] + jnp.dot(mixed, wo_ref[...], preferred_element_type=f32)
    hn = _rms(x, gf_ref[...]).astype(jnp.bfloat16)
    for c in range(D_FF // ff_chunk):
        u = jnp.dot(hn, wu_ref[:, c * ff_chunk:(c + 1) * ff_chunk], preferred_element_type=f32)
        u = jnp.square(jnp.maximum(u, 0.0)).astype(jnp.bfloat16)
        x = x + jnp.dot(u, wd_ref[c * ff_chunk:(c + 1) * ff_chunk, :], preferred_element_type=f32)
    if final_norm:
        x = _rms(x, gn_ref[...])
    o_ref[...] = x


def _out_ffn(x, yl, ys, ya, wo, gf, wu, wd, gn, tm, final_norm, ff_chunk=1024):
    n = x.shape[0]
    row = lambda width: pl.BlockSpec((tm, width), lambda i: (i, 0))
    return pl.pallas_call(
        functools.partial(_out_ffn_kernel, ff_chunk=ff_chunk, final_norm=final_norm),
        grid=(n // tm,),
        in_specs=[row(D_MODEL), row(LRU_WIDTH), row(SSD_WIDTH), row(ATT_WIDTH),
                  _resident((D_MODEL, D_MODEL)), _resident((1, D_MODEL)),
                  _resident((D_MODEL, D_FF)), _resident((D_FF, D_MODEL)), _resident((1, D_MODEL))],
        out_specs=row(D_MODEL),
        out_shape=jax.ShapeDtypeStruct((n, D_MODEL), jnp.float32),
        compiler_params=_params(_ARB),
        name="out_ffn",
    )(x, yl, ys, ya, wo, gf, wu, wd, gn)


def _prep_w_in(w_in):
    o_dt = 2 * LRU_WIDTH + SSD_WIDTH + SSD_CONV_CH
    head = w_in[..., :o_dt]
    dt = jnp.pad(w_in[..., o_dt:o_dt + SSD_HEADS], ((0, 0), (0, 0), (0, DT_PAD - SSD_HEADS)))
    tail = w_in[..., o_dt + SSD_HEADS:]
    return jnp.concatenate([head, dt, tail], axis=-1).astype(jnp.bfloat16)


def _block_diag(w):
    eye = jnp.eye(LRU_BLOCKS, dtype=w.dtype)
    full = w[:, :, :, None, :] * eye[None, :, None, :, None]
    return full.reshape(w.shape[0], LRU_WIDTH, LRU_WIDTH)


def _pad_heads(v):
    return jnp.pad(v, ((0, 0), (0, DT_PAD - SSD_HEADS)))[:, None, :]


def _layer(x, states, p, lam_init, bsz, tiles, attn, final_norm):
    tm, t_lru, t_ssd = tiles
    conv_lru0, h_lru0, conv_ssd0, h_ssd0 = states
    lru_in, ssd_in, q, k32, v32, kb, vb = _in_proj(x, p["norm_mix_g"], p["w_in"], tm)
    y_lru, conv_lru, h_lru = _lru(lru_in, conv_lru0, h_lru0, p["lru_conv_w"], p["lru_conv_b"],
                                  p["lru_wg"], p["lru_bg"], p["lru_nsp"], bsz, t_lru)
    y_ssd, conv_ssd, h_ssd = _ssd(ssd_in, conv_ssd0, h_ssd0, p["ssd_conv_w"], p["ssd_conv_b"],
                                  p["ssd_dt_bias"], p["ssd_a_neg"], p["ssd_d"], p["ssd_norm_g"], bsz, t_ssd)
    y_att = attn(q, kb, vb)
    x = _out_ffn(x, y_lru, y_ssd, y_att, p["w_out"], p["norm_ffn_g"], p["w_up"], p["w_down"],
                 p["norm_f_g"], tm, final_norm)
    return x, (k32, v32, conv_lru, h_lru, conv_ssd, h_ssd)


def _forward(x_prompt, x_sample, cache_att_k, cache_att_v, state_lru_conv, state_lru_h,
             state_ssd_conv, state_ssd_h, norm_mix_g, w_in, lru_conv_w, lru_conv_b,
             lru_wa, lru_ba, lru_wx, lru_bx, lru_lambda, ssd_conv_w, ssd_conv_b,
             ssd_dt_bias, ssd_a_log, ssd_d, ssd_norm_g, att_lambda, att_subln_g,
             w_out, norm_ffn_g, w_up, w_down, norm_f_g, *, tiles_p, tiles_s, bq):
    f32, bf16 = jnp.float32, jnp.bfloat16
    depth = w_in.shape[0]
    bp, seq, _ = x_prompt.shape
    bs, lq, _ = x_sample.shape
    past_len = cache_att_k.shape[2]
    gw = 2 * SSD_HEAD_DIM

    stacked = {
        "norm_mix_g": norm_mix_g[:, None, :],
        "w_in": _prep_w_in(w_in),
        "lru_conv_w": lru_conv_w, "lru_conv_b": lru_conv_b[:, None, :],
        "lru_wg": jnp.concatenate([_block_diag(lru_wa), _block_diag(lru_wx)], axis=-1).astype(bf16),
        "lru_bg": jnp.concatenate([lru_ba, lru_bx], axis=-1)[:, None, :],
        "lru_nsp": (-LRU_C * jax.nn.softplus(-lru_lambda.astype(f32)))[:, None, :],
        "ssd_conv_w": ssd_conv_w, "ssd_conv_b": ssd_conv_b[:, None, :],
        "ssd_dt_bias": _pad_heads(ssd_dt_bias),
        "ssd_a_neg": _pad_heads(-jnp.exp(ssd_a_log.astype(f32))),
        "ssd_d": jnp.repeat(ssd_d, SSD_HEAD_DIM, axis=-1)[:, None, :],
        "ssd_norm_g": ssd_norm_g[:, None, :],
        "att_lambda": att_lambda, "att_subln_g": att_subln_g[:, None, :],
        "w_out": w_out.astype(bf16), "norm_ffn_g": norm_ffn_g[:, None, :],
        "w_up": w_up.astype(bf16), "w_down": w_down.astype(bf16),
    }
    gn = norm_f_g[None, :]

    xp = x_prompt.reshape(bp * seq, D_MODEL)
    xs = x_sample.reshape(bs * lq, D_MODEL)
    past_k = cache_att_k.reshape(depth * bs * past_len, ATT_WIDTH)
    past_v = cache_att_v.reshape(depth * bs * past_len, ATT_WIDTH)
    zero_states = (jnp.zeros((bp, CONV_W - 1, LRU_WIDTH), f32), jnp.zeros((bp, 1, LRU_WIDTH), f32),
                   jnp.zeros((bp, CONV_W - 1, SSD_CONV_CH), f32),
                   jnp.zeros((bp, SSD_GROUPS, gw, SSD_STATE), f32))
    st_p, st_s = [], []
    for l in range(depth):
        p = {name: v[l] for name, v in stacked.items()}
        p["norm_f_g"] = gn
        lam_init = 0.8 - 0.6 * math.exp(-0.3 * l)
        last = l == depth - 1
        attn_p = functools.partial(_attn_prompt, lv=p["att_lambda"], g=p["att_subln_g"],
                                   lam_init=lam_init, bsz=bp, bq=bq)
        xp, sp = _layer(xp, zero_states, p, lam_init, bp, tiles_p, attn_p, last)
        states_s = (state_lru_conv[l], state_lru_h[l][:, None, :], state_ssd_conv[l],
                    state_ssd_h[l].reshape(bs, SSD_GROUPS, gw, SSD_STATE))
        attn_s = functools.partial(_attn_sample, past_k=past_k, past_v=past_v, layer=l,
                                   lv=p["att_lambda"], g=p["att_subln_g"], lam_init=lam_init,
                                   bsz=bs, lq=lq, past_len=past_len)
        xs, ss = _layer(xs, states_s, p, lam_init, bs, tiles_s, attn_s, last)
        st_p.append(sp)
        st_s.append(ss)

    def collect(sts, bsz, length):
        stack = lambda i: jnp.stack([s[i] for s in sts], axis=0)
        return (stack(0).reshape(depth, bsz, length, ATT_HEADS, ATT_V_DIM),
                stack(1).reshape(depth, bsz, length, ATT_HEADS, ATT_V_DIM),
                stack(2), stack(3).reshape(depth, bsz, LRU_WIDTH), stack(4),
                stack(5).reshape(depth, bsz, SSD_HEADS, SSD_HEAD_DIM, SSD_STATE))

    return ((xp.reshape(bp, seq, D_MODEL), xs.reshape(bs, lq, D_MODEL))
            + collect(st_p, bp, seq) + collect(st_s, bs, lq))


def kernel(x_prompt, x_sample, cache_att_k, cache_att_v, state_lru_conv, state_lru_h, state_ssd_conv, state_ssd_h, norm_mix_g, w_in, lru_conv_w, lru_conv_b, lru_wa, lru_ba, lru_wx, lru_bx, lru_lambda, ssd_conv_w, ssd_conv_b, ssd_dt_bias, ssd_a_log, ssd_d, ssd_norm_g, att_lambda, att_subln_g, w_out, norm_ffn_g, w_up, w_down, norm_f_g):
    lq = x_sample.shape[1]
    n_s = x_sample.shape[0] * lq
    return _forward(x_prompt, x_sample, cache_att_k, cache_att_v, state_lru_conv, state_lru_h,
                    state_ssd_conv, state_ssd_h, norm_mix_g, w_in, lru_conv_w, lru_conv_b,
                    lru_wa, lru_ba, lru_wx, lru_bx, lru_lambda, ssd_conv_w, ssd_conv_b,
                    ssd_dt_bias, ssd_a_log, ssd_d, ssd_norm_g, att_lambda, att_subln_g,
                    w_out, norm_ffn_g, w_up, w_down, norm_f_g,
                    tiles_p=(512, 512, 256), tiles_s=(n_s, lq, lq), bq=256)
```

```python
import functools
import math

import jax
import jax.numpy as jnp
from jax import lax
from jax.experimental import pallas as pl
from jax.experimental.pallas import tpu as pltpu

D_MODEL = 1024
DEPTH = 4
CHUNK = 64
CONV_W = 4
LRU_WIDTH = 256
LRU_BLOCKS = 4
LRU_BLOCK = LRU_WIDTH // LRU_BLOCKS
LRU_C = 8.0
SSD_WIDTH = 256
SSD_HEAD_DIM = 64
SSD_HEADS = 4
SSD_GROUPS = 2
SSD_STATE = 128
SSD_CONV_CH = SSD_WIDTH + 2 * SSD_GROUPS * SSD_STATE
ATT_WIDTH = 512
ATT_HEADS = 4
ATT_V_DIM = 128
ATT_HEAD_DIM = 64
ATT_SCALE = ATT_HEAD_DIM ** -0.5
D_FF = 4 * D_MODEL
EPS = 1e-6

LANES = 128
CONV_PAD = 8
DT_PAD = LANES
LRU_IN_W = 2 * LRU_WIDTH
SSD_IN_W = SSD_WIDTH + SSD_CONV_CH + DT_PAD
IN_COLS_PADDED = LRU_IN_W + SSD_IN_W + 3 * ATT_WIDTH
NEG_BIG = -1e30
VMEM_LIMIT = 56 * 1024 * 1024

_ARB = "arbitrary"


def _params(*sem):
    return pltpu.CompilerParams(dimension_semantics=sem, vmem_limit_bytes=VMEM_LIMIT)


def _resident(shape):
    nd = len(shape)
    return pl.BlockSpec(shape, lambda *_: (0,) * nd, pipeline_mode=pl.Buffered(1))


def _rms(x, g):
    return x * lax.rsqrt(jnp.mean(x * x, axis=-1, keepdims=True) + EPS) * g


def _in_proj_kernel(x_ref, g_ref, w_ref, wvt_ref, lru_ref, ssd_ref, q_ref, k_ref, v_ref, kb_ref, vb_ref):
    xn = _rms(x_ref[...], g_ref[...]).astype(jnp.bfloat16)

    def seg(lo, width):
        return jnp.dot(xn, w_ref[:, lo:lo + width], preferred_element_type=jnp.float32)

    lru_ref[...] = seg(0, LRU_IN_W)
    ssd_ref[...] = seg(LRU_IN_W, SSD_IN_W)
    lo = LRU_IN_W + SSD_IN_W
    q_ref[...] = (seg(lo, ATT_WIDTH) * ATT_SCALE).astype(jnp.bfloat16)
    k = seg(lo + ATT_WIDTH, ATT_WIDTH)
    k_ref[...] = k
    kb_ref[...] = k.astype(jnp.bfloat16)
    v = seg(lo + 2 * ATT_WIDTH, ATT_WIDTH)
    v_ref[...] = v
    if len(vb_ref.shape) == 2:
        vb_ref[...] = v.astype(jnp.bfloat16)
    else:
        vt = lax.dot_general(wvt_ref[...], xn, (((1,), (1,)), ((), ())),
                             preferred_element_type=jnp.float32).astype(jnp.bfloat16)
        bk = vb_ref.shape[2]
        for c in range(vb_ref.shape[0]):
            vb_ref[c] = vt[:, c * bk:(c + 1) * bk]


def _in_proj(x, g, w, wvt, tm, bk):
    n = x.shape[0]
    row = lambda width: pl.BlockSpec((tm, width), lambda i: (i, 0))
    f32, bf16 = jnp.float32, jnp.bfloat16
    if bk is None:
        vb_spec, vb_shape = row(ATT_WIDTH), jax.ShapeDtypeStruct((n, ATT_WIDTH), bf16)
    else:
        vb_spec = pl.BlockSpec((tm // bk, ATT_WIDTH, bk), lambda i: (i, 0, 0))
        vb_shape = jax.ShapeDtypeStruct((n // bk, ATT_WIDTH, bk), bf16)
    return pl.pallas_call(
        _in_proj_kernel,
        grid=(n // tm,),
        in_specs=[row(D_MODEL), _resident((1, D_MODEL)), _resident((D_MODEL, IN_COLS_PADDED)),
                  _resident((ATT_WIDTH, D_MODEL))],
        out_specs=[row(LRU_IN_W), row(SSD_IN_W)] + [row(ATT_WIDTH)] * 4 + [vb_spec],
        out_shape=[jax.ShapeDtypeStruct((n, LRU_IN_W), f32), jax.ShapeDtypeStruct((n, SSD_IN_W), f32),
                   jax.ShapeDtypeStruct((n, ATT_WIDTH), bf16),
                   jax.ShapeDtypeStruct((n, ATT_WIDTH), f32), jax.ShapeDtypeStruct((n, ATT_WIDTH), f32),
                   jax.ShapeDtypeStruct((n, ATT_WIDTH), bf16), vb_shape],
        compiler_params=_params(_ARB),
        name="in_proj",
    )(x, g, w, wvt)


def _causal_conv(xp_ref, x, prev_ref, w_ref, b_ref, first):
    t = x.shape[0]
    keep = CONV_W - 1

    @pl.when(first)
    def _():
        xp_ref[CONV_PAD - keep:CONV_PAD, :] = prev_ref[0]

    @pl.when(jnp.logical_not(first))
    def _():
        xp_ref[CONV_PAD - keep:CONV_PAD, :] = xp_ref[CONV_PAD + t - keep:CONV_PAD + t, :]

    xp_ref[CONV_PAD:CONV_PAD + t, :] = x
    y = b_ref[...] + w_ref[CONV_W - 1:CONV_W, :] * x
    for j in range(CONV_W - 1):
        y = y + w_ref[j:j + 1, :] * xp_ref[CONV_PAD - keep + j:CONV_PAD - keep + j + t, :]
    return y


def _row_scan(step, vals, t):
    row = lax.broadcasted_iota(jnp.int32, (t, 1), 0)
    d = 1
    while d < t:
        shifted = tuple(pltpu.roll(v, d, 0) for v in vals)
        new = step(shifted, vals)
        vals = tuple(jnp.where(row >= d, n, v) for n, v in zip(new, vals))
        d *= 2
    return vals


def _lru_kernel(in_ref, conv0_ref, h0_ref, cw_ref, cb_ref, wg_ref, bg_ref, nsp_ref,
                y_ref, conv_out_ref, h_out_ref, xp_ref, h_ref):
    i = pl.program_id(1)
    first = i == 0
    t = in_ref.shape[0]

    @pl.when(first)
    def _():
        h_ref[...] = h0_ref[0]

    gate = in_ref[:, 0:LRU_WIDTH]
    x = _causal_conv(xp_ref, in_ref[:, LRU_WIDTH:LRU_IN_W], conv0_ref, cw_ref, cb_ref, first)
    pre = jnp.dot(x.astype(jnp.bfloat16), wg_ref[...], preferred_element_type=jnp.float32) + bg_ref[...]
    r = jax.nn.sigmoid(pre[:, 0:LRU_WIDTH])
    ig = jax.nn.sigmoid(pre[:, LRU_WIDTH:2 * LRU_WIDTH])
    log_a = r * nsp_ref[...]
    a = jnp.exp(log_a)
    b = jnp.sqrt(-jnp.tanh(log_a) * (a * a + 1.0)) * (ig * x)

    def step(shifted, cur):
        a_s, b_s = shifted
        a_c, b_c = cur
        return a_c * a_s, a_c * b_s + b_c

    a_cum, b_cum = _row_scan(step, (a, b), t)
    h = a_cum * h_ref[...] + b_cum
    h_ref[...] = h[t - 1:t, :]
    g = gate
    gelu = 0.5 * g * (1.0 + jnp.tanh(math.sqrt(2.0 / math.pi) * (g + 0.044715 * (g * g * g))))
    y_ref[...] = (gelu * h).astype(y_ref.dtype)

    @pl.when(i == pl.num_programs(1) - 1)
    def _():
        keep = CONV_W - 1
        conv_out_ref[0] = xp_ref[CONV_PAD + t - keep:CONV_PAD + t, :]
        h_out_ref[0] = h[t - 1:t, :]


def _lru(lru_in, conv0, h0, cw, cb, wg, bg, nsp, bsz, t):
    n = lru_in.shape[0]
    nt = n // bsz // t
    keep = CONV_W - 1
    per_b = lambda shape: pl.BlockSpec((1,) + shape, lambda b, i: (b,) + (0,) * len(shape))
    f32 = jnp.float32
    return pl.pallas_call(
        _lru_kernel,
        grid=(bsz, nt),
        in_specs=[pl.BlockSpec((t, LRU_IN_W), lambda b, i: (b * nt + i, 0)),
                  per_b((keep, LRU_WIDTH)), per_b((1, LRU_WIDTH)),
                  _resident((CONV_W, LRU_WIDTH)), _resident((1, LRU_WIDTH)),
                  _resident((LRU_WIDTH, 2 * LRU_WIDTH)), _resident((1, 2 * LRU_WIDTH)),
                  _resident((1, LRU_WIDTH))],
        out_specs=[pl.BlockSpec((t, LRU_WIDTH), lambda b, i: (b * nt + i, 0)),
                   per_b((keep, LRU_WIDTH)), per_b((1, LRU_WIDTH))],
        out_shape=[jax.ShapeDtypeStruct((n, LRU_WIDTH), jnp.bfloat16),
                   jax.ShapeDtypeStruct((bsz, keep, LRU_WIDTH), f32),
                   jax.ShapeDtypeStruct((bsz, 1, LRU_WIDTH), f32)],
        scratch_shapes=[pltpu.VMEM((CONV_PAD + t, LRU_WIDTH), f32), pltpu.VMEM((1, LRU_WIDTH), f32)],
        compiler_params=_params(_ARB, _ARB),
        name="lru",
    )(lru_in, conv0, h0, cw, cb, wg, bg, nsp)


def _ssd_kernel(in_ref, conv0_ref, h0_ref, cw_ref, cb_ref, dtb_ref, aneg_ref, dskip_ref, ng_ref,
                y_ref, conv_out_ref, h_out_ref, xp_ref, h_ref):
    i = pl.program_id(1)
    first = i == 0
    t = in_ref.shape[0]
    bf16 = jnp.bfloat16
    gw = 2 * SSD_HEAD_DIM

    @pl.when(first)
    def _():
        h_ref[...] = h0_ref[0]

    z = in_ref[:, 0:SSD_WIDTH]
    xc = _causal_conv(xp_ref, in_ref[:, SSD_WIDTH:SSD_WIDTH + SSD_CONV_CH], conv0_ref, cw_ref, cb_ref, first)
    xc = xc * jax.nn.sigmoid(xc)
    dt = jax.nn.softplus(in_ref[:, SSD_WIDTH + SSD_CONV_CH:SSD_IN_W] + dtb_ref[...])
    (cs,) = _row_scan(lambda s, c: (s[0] + c[0],), (dt * aneg_ref[...],), t)
    cs_t = cs.T
    cs_last = cs[t - 1:t, :]
    tail = jnp.exp(cs_last - cs)
    grow = jnp.exp(cs)
    chunk_decay = jnp.exp(cs_last)

    lane = lax.broadcasted_iota(jnp.int32, (t, gw), 1)
    low = lane < SSD_HEAD_DIM
    srow = lax.broadcasted_iota(jnp.int32, (gw, 1), 0) < SSD_HEAD_DIM
    causal = lax.broadcasted_iota(jnp.int32, (t, t), 0) >= lax.broadcasted_iota(jnp.int32, (t, t), 1)

    def pick(arr, g):
        c0 = jnp.broadcast_to(arr[:, 2 * g:2 * g + 1], (t, gw))
        c1 = jnp.broadcast_to(arr[:, 2 * g + 1:2 * g + 2], (t, gw))
        return jnp.where(low, c0, c1)

    ys = []
    for g in range(SSD_GROUPS):
        xs = xc[:, g * gw:(g + 1) * gw]
        bm = xc[:, SSD_WIDTH + g * SSD_STATE:SSD_WIDTH + (g + 1) * SSD_STATE].astype(bf16)
        cm = xc[:, SSD_WIDTH + (SSD_GROUPS + g) * SSD_STATE:SSD_WIDTH + (SSD_GROUPS + g + 1) * SSD_STATE].astype(bf16)
        xdt = xs * pick(dt, g)
        xdt_b = xdt.astype(bf16)
        cb = lax.dot_general(cm, bm, (((1,), (1,)), ((), ())), preferred_element_type=jnp.float32)
        y_heads = []
        for hh in range(2):
            hd = 2 * g + hh
            seg = jnp.broadcast_to(cs[:, hd:hd + 1], (t, t)) - cs_t[hd:hd + 1, :]
            scores = (cb * jnp.exp(jnp.where(causal, seg, NEG_BIG))).astype(bf16)
            y_heads.append(jnp.dot(scores, xdt_b, preferred_element_type=jnp.float32))
        y_intra = jnp.where(low, y_heads[0], y_heads[1])
        h_g = h_ref[g]
        y_inter = lax.dot_general(cm, h_g.astype(bf16), (((1,), (1,)), ((), ())),
                                  preferred_element_type=jnp.float32) * pick(grow, g)
        s_local = lax.dot_general((xdt * pick(tail, g)).astype(bf16), bm, (((0,), (0,)), ((), ())),
                                  preferred_element_type=jnp.float32)
        dec = jnp.where(srow, chunk_decay[:, 2 * g:2 * g + 1], chunk_decay[:, 2 * g + 1:2 * g + 2])
        h_ref[g] = dec * h_g + s_local
        y = y_intra + y_inter + dskip_ref[:, g * gw:(g + 1) * gw] * xs
        zg = z[:, g * gw:(g + 1) * gw]
        y = y * (zg * jax.nn.sigmoid(zg))
        ys.append(_rms(y, ng_ref[:, g * gw:(g + 1) * gw]))
    y_ref[...] = jnp.concatenate(ys, axis=1).astype(y_ref.dtype)

    @pl.when(i == pl.num_programs(1) - 1)
    def _():
        keep = CONV_W - 1
        conv_out_ref[0] = xp_ref[CONV_PAD + t - keep:CONV_PAD + t, :]
        h_out_ref[0] = h_ref[...]


def _ssd(ssd_in, conv0, h0, cw, cb, dtb, aneg, dskip, ng, bsz, t):
    n = ssd_in.shape[0]
    nt = n // bsz // t
    keep = CONV_W - 1
    gw = 2 * SSD_HEAD_DIM
    per_b = lambda shape: pl.BlockSpec((1,) + shape, lambda b, i: (b,) + (0,) * len(shape))
    f32 = jnp.float32
    return pl.pallas_call(
        _ssd_kernel,
        grid=(bsz, nt),
        in_specs=[pl.BlockSpec((t, SSD_IN_W), lambda b, i: (b * nt + i, 0)),
                  per_b((keep, SSD_CONV_CH)), per_b((SSD_GROUPS, gw, SSD_STATE)),
                  _resident((CONV_W, SSD_CONV_CH)), _resident((1, SSD_CONV_CH)),
                  _resident((1, DT_PAD)), _resident((1, DT_PAD)),
                  _resident((1, SSD_WIDTH)), _resident((1, SSD_WIDTH))],
        out_specs=[pl.BlockSpec((t, SSD_WIDTH), lambda b, i: (b * nt + i, 0)),
                   per_b((keep, SSD_CONV_CH)), per_b((SSD_GROUPS, gw, SSD_STATE))],
        out_shape=[jax.ShapeDtypeStruct((n, SSD_WIDTH), jnp.bfloat16),
                   jax.ShapeDtypeStruct((bsz, keep, SSD_CONV_CH), f32),
                   jax.ShapeDtypeStruct((bsz, SSD_GROUPS, gw, SSD_STATE), f32)],
        scratch_shapes=[pltpu.VMEM((CONV_PAD + t, SSD_CONV_CH), f32),
                        pltpu.VMEM((SSD_GROUPS, gw, SSD_STATE), f32)],
        compiler_params=_params(_ARB, _ARB),
        name="ssd",
    )(ssd_in, conv0, h0, cw, cb, dtb, aneg, dskip, ng)


def _stack_q(q):
    lane = lax.broadcasted_iota(jnp.int32, q.shape, 1)
    zero = jnp.zeros_like(q)
    return jnp.concatenate([jnp.where(lane < ATT_HEAD_DIM, q, zero),
                            jnp.where(lane < ATT_HEAD_DIM, zero, q)], axis=0)


def _diff_lambda(lv, lam_init):
    e1 = jnp.exp(jnp.sum(lv[0:1, :] * lv[1:2, :], axis=1, keepdims=True))
    e2 = jnp.exp(jnp.sum(lv[2:3, :] * lv[3:4, :], axis=1, keepdims=True))
    return e1 - e2 + lam_init


def _attn_finish(acc, l, lam, g, lam_init, bq):
    o = acc[0:bq] / l[0:bq] - lam * (acc[bq:2 * bq] / l[bq:2 * bq])
    return _rms(o, g) * (1.0 - lam_init)


def _attn_prompt_kernel(q_ref, k_ref, vt_ref, lv_ref, g_ref, o_ref, acc0_ref, acc1_ref, s0_ref, s1_ref,
                        *, lam_init, bq):
    i = pl.program_id(2)
    accs = (acc0_ref, acc1_ref)
    s_refs = (s0_ref, s1_ref)
    qs = _stack_q(q_ref[...])
    qs = (qs[0:bq], qs[bq:2 * bq])
    for a in accs:
        a[...] = jnp.zeros(a.shape, jnp.float32)

    def scores(j):
        k = k_ref[pl.ds(pl.multiple_of(j * bq, bq), bq), :]
        return tuple(lax.dot_general(k, qs[mp], (((1,), (1,)), ((), ())),
                                     preferred_element_type=jnp.float32) for mp in range(2))

    def consume(j, mask, stats):
        vt = vt_ref[j]
        out = []
        for mp in range(2):
            m_old, l_old = stats[mp]
            s = s_refs[mp][...]
            if mask is not None:
                s = jnp.where(mask, s, NEG_BIG)
            m_new = jnp.maximum(m_old, jnp.max(s, axis=0, keepdims=True))
            p = jnp.exp(s - m_new)
            alpha = jnp.exp(m_old - m_new)
            l_new = alpha * l_old + jnp.sum(p, axis=0, keepdims=True)
            accs[mp][...] = alpha * accs[mp][...] + jnp.dot(vt, p.astype(jnp.bfloat16),
                                                            preferred_element_type=jnp.float32)
            out.append((m_new, l_new))
        return tuple(out)

    def store_scores(s):
        for mp in range(2):
            s_refs[mp][...] = s[mp]

    store_scores(scores(0))

    def body(j, stats):
        s_next = scores(j + 1)
        stats = consume(j, None, stats)
        store_scores(s_next)
        return stats

    init = ((jnp.full((1, bq), NEG_BIG, jnp.float32), jnp.zeros((1, bq), jnp.float32)),) * 2
    stats = lax.fori_loop(0, i, body, init)
    kpos = lax.broadcasted_iota(jnp.int32, (bq, bq), 0)
    qpos = lax.broadcasted_iota(jnp.int32, (bq, bq), 1)
    stats = consume(i, (kpos // CHUNK) <= (qpos // CHUNK), stats)
    lam = _diff_lambda(lv_ref[...], lam_init)
    o = acc0_ref[...] / stats[0][1] - lam * (acc1_ref[...] / stats[1][1])
    o = o * lax.rsqrt(jnp.mean(o * o, axis=0, keepdims=True) + EPS) * g_ref[...] * (1.0 - lam_init)
    o_ref[...] = o.T.astype(o_ref.dtype)


def _attn_prompt(q, kb, vt, lv, g, lam_init, bsz, bq):
    n = q.shape[0]
    seq = n // bsz
    nq = seq // bq
    assert bq % CHUNK == 0 and seq % bq == 0 and vt.shape == (n // bq, ATT_WIDTH, bq)
    f32 = jnp.float32
    return pl.pallas_call(
        functools.partial(_attn_prompt_kernel, lam_init=lam_init, bq=bq),
        grid=(bsz, ATT_HEADS, nq),
        in_specs=[pl.BlockSpec((bq, ATT_V_DIM), lambda b, h, i: (b * nq + i, h)),
                  pl.BlockSpec((seq, ATT_V_DIM), lambda b, h, i: (b, h)),
                  pl.BlockSpec((nq, ATT_V_DIM, bq), lambda b, h, i: (b, h, 0)),
                  pl.BlockSpec((4, ATT_HEAD_DIM), lambda b, h, i: (0, 0)),
                  pl.BlockSpec((ATT_V_DIM, 1), lambda b, h, i: (0, 0))],
        out_specs=pl.BlockSpec((bq, ATT_V_DIM), lambda b, h, i: (b * nq + i, h)),
        out_shape=jax.ShapeDtypeStruct((n, ATT_WIDTH), jnp.bfloat16),
        scratch_shapes=[pltpu.VMEM((ATT_V_DIM, bq), f32), pltpu.VMEM((ATT_V_DIM, bq), f32),
                        pltpu.VMEM((bq, bq), f32), pltpu.VMEM((bq, bq), f32)],
        compiler_params=_params(_ARB, _ARB, _ARB),
        name="attn_prompt",
    )(q, kb, vt, lv, g)


def _attn_sample_kernel(q_ref, kn_ref, vn_ref, kp_ref, vp_ref, lv_ref, g_ref, o_ref, *, lam_init, lq):
    lam = _diff_lambda(lv_ref[...], lam_init)
    bf16 = jnp.bfloat16
    nt = (((1,), (1,)), ((), ()))
    outs = []
    for h in range(ATT_HEADS):
        sl = slice(h * ATT_V_DIM, (h + 1) * ATT_V_DIM)
        qs = _stack_q(q_ref[:, sl])
        kp = kp_ref[:, sl].astype(bf16)
        vp = vp_ref[:, sl].astype(bf16)
        s_p = lax.dot_general(qs, kp, nt, preferred_element_type=jnp.float32)
        s_n = lax.dot_general(qs, kn_ref[:, sl], nt, preferred_element_type=jnp.float32)
        m = jnp.maximum(jnp.max(s_p, axis=1, keepdims=True), jnp.max(s_n, axis=1, keepdims=True))
        p_p = jnp.exp(s_p - m)
        p_n = jnp.exp(s_n - m)
        l = jnp.sum(p_p, axis=1, keepdims=True) + jnp.sum(p_n, axis=1, keepdims=True)
        acc = (jnp.dot(p_p.astype(bf16), vp, preferred_element_type=jnp.float32)
               + jnp.dot(p_n.astype(bf16), vn_ref[:, sl], preferred_element_type=jnp.float32))
        outs.append(_attn_finish(acc, l, lam, g_ref[...], lam_init, lq))
    o_ref[...] = jnp.concatenate(outs, axis=1).astype(o_ref.dtype)


def _attn_sample(q, kb, vb, past_k, past_v, layer, lv, g, lam_init, bsz, lq, past_len):
    assert past_len % CHUNK == 0 and lq <= CHUNK
    n = q.shape[0]
    step = lambda: pl.BlockSpec((lq, ATT_WIDTH), lambda b: (b, 0))
    past = lambda: pl.BlockSpec((past_len, ATT_WIDTH), lambda b: (layer * bsz + b, 0))
    return pl.pallas_call(
        functools.partial(_attn_sample_kernel, lam_init=lam_init, lq=lq),
        grid=(bsz,),
        in_specs=[step(), step(), step(), past(), past(),
                  pl.BlockSpec((4, ATT_HEAD_DIM), lambda b: (0, 0)),
                  pl.BlockSpec((1, ATT_V_DIM), lambda b: (0, 0))],
        out_specs=step(),
        out_shape=jax.ShapeDtypeStruct((n, ATT_WIDTH), jnp.bfloat16),
        compiler_params=_params(_ARB),
        name="attn_sample",
    )(q, kb, vb, past_k, past_v, lv, g)


def _out_ffn_kernel(x_ref, yl_ref, ys_ref, ya_ref, wo_ref, gf_ref, wu_ref, wd_ref, gn_ref, o_ref,
                    *, ff_chunk, final_norm):
    f32 = jnp.float32
    mixed = jnp.concatenate([yl_ref[...], ys_ref[...], ya_ref[...]], axis=1)
    x = x_ref[...] + jnp.dot(mixed, wo_ref[...], preferred_element_type=f32)
    hn = _rms(x, gf_ref[...]).astype(jnp.bfloat16)
    for c in range(D_FF // ff_chunk):
        u = jnp.dot(hn, wu_ref[:, c * ff_chunk:(c + 1) * ff_chunk], preferred_element_type=f32)
        u = jnp.square(jnp.maximum(u, 0.0)).astype(jnp.bfloat16)
        x = x + jnp.dot(u, wd_ref[c * ff_chunk:(c + 1) * ff_chunk, :], preferred_element_type=f32)
    if final_norm:
        x = _rms(x, gn_ref[...])
    o_ref[...] = x


def _out_ffn(x, yl, ys, ya, wo, gf, wu, wd, gn, tm, final_norm, ff_chunk=1024):
    n = x.shape[0]
    row = lambda width: pl.BlockSpec((tm, width), lambda i: (i, 0))
    return pl.pallas_call(
        functools.partial(_out_ffn_kernel, ff_chunk=ff_chunk, final_norm=final_norm),
        grid=(n // tm,),
        in_specs=[row(D_MODEL), row(LRU_WIDTH), row(SSD_WIDTH), row(ATT_WIDTH),
                  _resident((D_MODEL, D_MODEL)), _resident((1, D_MODEL)),
                  _resident((D_MODEL, D_FF)), _resident((D_FF, D_MODEL)), _resident((1, D_MODEL))],
        out_specs=row(D_MODEL),
        out_shape=jax.ShapeDtypeStruct((n, D_MODEL), jnp.float32),
        compiler_params=_params(_ARB),
        name="out_ffn",
    )(x, yl, ys, ya, wo, gf, wu, wd, gn)


def _prep_w_in(w_in):
    o_dt = 2 * LRU_WIDTH + SSD_WIDTH + SSD_CONV_CH
    head = w_in[..., :o_dt]
    dt = jnp.pad(w_in[..., o_dt:o_dt + SSD_HEADS], ((0, 0), (0, 0), (0, DT_PAD - SSD_HEADS)))
    tail = w_in[..., o_dt + SSD_HEADS:]
    return jnp.concatenate([head, dt, tail], axis=-1).astype(jnp.bfloat16)


def _block_diag(w):
    eye = jnp.eye(LRU_BLOCKS, dtype=w.dtype)
    full = w[:, :, :, None, :] * eye[None, :, None, :, None]
    return full.reshape(w.shape[0], LRU_WIDTH, LRU_WIDTH)


def _pad_heads(v):
    return jnp.pad(v, ((0, 0), (0, DT_PAD - SSD_HEADS)))[:, None, :]


def _layer(x, states, p, lam_init, bsz, tiles, attn, vt_block, final_norm):
    tm, t_lru, t_ssd = tiles
    conv_lru0, h_lru0, conv_ssd0, h_ssd0 = states
    lru_in, ssd_in, q, k32, v32, kb, vb = _in_proj(x, p["norm_mix_g"], p["w_in"], p["w_vt"], tm, vt_block)
    y_lru, conv_lru, h_lru = _lru(lru_in, conv_lru0, h_lru0, p["lru_conv_w"], p["lru_conv_b"],
                                  p["lru_wg"], p["lru_bg"], p["lru_nsp"], bsz, t_lru)
    y_ssd, conv_ssd, h_ssd = _ssd(ssd_in, conv_ssd0, h_ssd0, p["ssd_conv_w"], p["ssd_conv_b"],
                                  p["ssd_dt_bias"], p["ssd_a_neg"], p["ssd_d"], p["ssd_norm_g"], bsz, t_ssd)
    y_att = attn(q, kb, vb)
    x = _out_ffn(x, y_lru, y_ssd, y_att, p["w_out"], p["norm_ffn_g"], p["w_up"], p["w_down"],
                 p["norm_f_g"], tm, final_norm)
    return x, (k32, v32, conv_lru, h_lru, conv_ssd, h_ssd)


def _forward(x_prompt, x_sample, cache_att_k, cache_att_v, state_lru_conv, state_lru_h,
             state_ssd_conv, state_ssd_h, norm_mix_g, w_in, lru_conv_w, lru_conv_b,
             lru_wa, lru_ba, lru_wx, lru_bx, lru_lambda, ssd_conv_w, ssd_conv_b,
             ssd_dt_bias, ssd_a_log, ssd_d, ssd_norm_g, att_lambda, att_subln_g,
             w_out, norm_ffn_g, w_up, w_down, norm_f_g, *, tiles_p, tiles_s, bq):
    f32, bf16 = jnp.float32, jnp.bfloat16
    depth = w_in.shape[0]
    bp, seq, _ = x_prompt.shape
    bs, lq, _ = x_sample.shape
    past_len = cache_att_k.shape[2]
    gw = 2 * SSD_HEAD_DIM

    stacked = {
        "norm_mix_g": norm_mix_g[:, None, :],
        "w_in": _prep_w_in(w_in),
        "w_vt": jnp.swapaxes(w_in[..., -ATT_WIDTH:], 1, 2).astype(bf16),
        "lru_conv_w": lru_conv_w, "lru_conv_b": lru_conv_b[:, None, :],
        "lru_wg": jnp.concatenate([_block_diag(lru_wa), _block_diag(lru_wx)], axis=-1).astype(bf16),
        "lru_bg": jnp.concatenate([lru_ba, lru_bx], axis=-1)[:, None, :],
        "lru_nsp": (-LRU_C * jax.nn.softplus(-lru_lambda.astype(f32)))[:, None, :],
        "ssd_conv_w": ssd_conv_w, "ssd_conv_b": ssd_conv_b[:, None, :],
        "ssd_dt_bias": _pad_heads(ssd_dt_bias),
        "ssd_a_neg": _pad_heads(-jnp.exp(ssd_a_log.astype(f32))),
        "ssd_d": jnp.repeat(ssd_d, SSD_HEAD_DIM, axis=-1)[:, None, :],
        "ssd_norm_g": ssd_norm_g[:, None, :],
        "att_lambda": att_lambda, "att_subln_g": att_subln_g[:, None, :],
        "att_subln_g_col": att_subln_g[:, :, None],
        "w_out": w_out.astype(bf16), "norm_ffn_g": norm_ffn_g[:, None, :],
        "w_up": w_up.astype(bf16), "w_down": w_down.astype(bf16),
    }
    gn = norm_f_g[None, :]

    xp = x_prompt.reshape(bp * seq, D_MODEL)
    xs = x_sample.reshape(bs * lq, D_MODEL)
    past_k = cache_att_k.reshape(depth * bs * past_len, ATT_WIDTH)
    past_v = cache_att_v.reshape(depth * bs * past_len, ATT_WIDTH)
    zero_states = (jnp.zeros((bp, CONV_W - 1, LRU_WIDTH), f32), jnp.zeros((bp, 1, LRU_WIDTH), f32),
                   jnp.zeros((bp, CONV_W - 1, SSD_CONV_CH), f32),
                   jnp.zeros((bp, SSD_GROUPS, gw, SSD_STATE), f32))
    st_p, st_s = [], []
    for l in range(depth):
        p = {name: v[l] for name, v in stacked.items()}
        p["norm_f_g"] = gn
        lam_init = 0.8 - 0.6 * math.exp(-0.3 * l)
        last = l == depth - 1
        attn_p = functools.partial(_attn_prompt, lv=p["att_lambda"], g=p["att_subln_g_col"],
                                   lam_init=lam_init, bsz=bp, bq=bq)
        xp, sp = _layer(xp, zero_states, p, lam_init, bp, tiles_p, attn_p, bq, last)
        states_s = (state_lru_conv[l], state_lru_h[l][:, None, :], state_ssd_conv[l],
                    state_ssd_h[l].reshape(bs, SSD_GROUPS, gw, SSD_STATE))
        attn_s = functools.partial(_attn_sample, past_k=past_k, past_v=past_v, layer=l,
                                   lv=p["att_lambda"], g=p["att_subln_g"], lam_init=lam_init,
                                   bsz=bs, lq=lq, past_len=past_len)
        xs, ss = _layer(xs, states_s, p, lam_init, bs, tiles_s, attn_s, None, last)
        st_p.append(sp)
        st_s.append(ss)

    def collect(sts, bsz, length):
        stack = lambda i: jnp.stack([s[i] for s in sts], axis=0)
        return (stack(0).reshape(depth, bsz, length, ATT_HEADS, ATT_V_DIM),
                stack(1).reshape(depth, bsz, length, ATT_HEADS, ATT_V_DIM),
                stack(2), stack(3).reshape(depth, bsz, LRU_WIDTH), stack(4),
                stack(5).reshape(depth, bsz, SSD_HEADS, SSD_HEAD_DIM, SSD_STATE))

    return ((xp.reshape(bp, seq, D_MODEL), xs.reshape(bs, lq, D_MODEL))
            + collect(st_p, bp, seq) + collect(st_s, bs, lq))


def kernel(x_prompt, x_sample, cache_att_k, cache_att_v, state_lru_conv, state_lru_h, state_ssd_conv, state_ssd_h, norm_mix_g, w_in, lru_conv_w, lru_conv_b, lru_wa, lru_ba, lru_wx, lru_bx, lru_lambda, ssd_conv_w, ssd_conv_b, ssd_dt_bias, ssd_a_log, ssd_d, ssd_norm_g, att_lambda, att_subln_g, w_out, norm_ffn_g, w_up, w_down, norm_f_g):
    lq = x_sample.shape[1]
    n_s = x_sample.shape[0] * lq
    return _forward(x_prompt, x_sample, cache_att_k, cache_att_v, state_lru_conv, state_lru_h,
                    state_ssd_conv, state_ssd_h, norm_mix_g, w_in, lru_conv_w, lru_conv_b,
                    lru_wa, lru_ba, lru_wx, lru_bx, lru_lambda, ssd_conv_w, ssd_conv_b,
                    ssd_dt_bias, ssd_a_log, ssd_d, ssd_norm_g, att_lambda, att_subln_g,
                    w_out, norm_ffn_g, w_up, w_down, norm_f_g,
                    tiles_p=(512, 512, 256), tiles_s=(n_s, lq, lq), bq=256)
```

```python
import functools
import math

import jax
import jax.numpy as jnp
from jax import lax
from jax.experimental import pallas as pl
from jax.experimental.pallas import tpu as pltpu

D_MODEL = 1024
DEPTH = 4
CHUNK = 64
CONV_W = 4
LRU_WIDTH = 256
LRU_BLOCKS = 4
LRU_BLOCK = LRU_WIDTH // LRU_BLOCKS
LRU_C = 8.0
SSD_WIDTH = 256
SSD_HEAD_DIM = 64
SSD_HEADS = 4
SSD_GROUPS = 2
SSD_STATE = 128
SSD_CONV_CH = SSD_WIDTH + 2 * SSD_GROUPS * SSD_STATE
ATT_WIDTH = 512
ATT_HEADS = 4
ATT_V_DIM = 128
ATT_HEAD_DIM = 64
ATT_SCALE = ATT_HEAD_DIM ** -0.5
LOG2_E = math.log2(math.e)
D_FF = 4 * D_MODEL
EPS = 1e-6

LANES = 128
BF16_SUBLANES = 16
ATT_VT_ROWS = ATT_V_DIM + BF16_SUBLANES
CONV_PAD = 8
DT_PAD = LANES
LRU_IN_W = 2 * LRU_WIDTH
SSD_IN_W = SSD_WIDTH + SSD_CONV_CH + DT_PAD
IN_COLS_PADDED = LRU_IN_W + SSD_IN_W + 3 * ATT_WIDTH
NEG_BIG = -1e30
VMEM_LIMIT = 56 * 1024 * 1024

_ARB = "arbitrary"


def _params(*sem):
    return pltpu.CompilerParams(dimension_semantics=sem, vmem_limit_bytes=VMEM_LIMIT)


def _resident(shape):
    nd = len(shape)
    return pl.BlockSpec(shape, lambda *_: (0,) * nd, pipeline_mode=pl.Buffered(1))


def _rms(x, g):
    return x * lax.rsqrt(jnp.mean(x * x, axis=-1, keepdims=True) + EPS) * g


def _in_proj_kernel(x_ref, g_ref, w_ref, wvt_ref, k_all_ref, v_all_ref,
                    lru_ref, ssd_ref, q_ref, k_ref, v_ref, kb_ref, vb_ref):
    del k_all_ref, v_all_ref
    tm = x_ref.shape[0]
    xn = _rms(x_ref[...], g_ref[...]).astype(jnp.bfloat16)

    def store_heads(ref, val):
        for h in range(ATT_HEADS):
            ref[pl.ds(h, tm, stride=ATT_HEADS), :] = val[:, h * ATT_V_DIM:(h + 1) * ATT_V_DIM]


    def seg(lo, width):
        return jnp.dot(xn, w_ref[:, lo:lo + width], preferred_element_type=jnp.float32)

    lru_ref[...] = seg(0, LRU_IN_W)
    ssd_ref[...] = seg(LRU_IN_W, SSD_IN_W)
    lo = LRU_IN_W + SSD_IN_W
    q_ref[...] = (seg(lo, ATT_WIDTH) * (ATT_SCALE * LOG2_E)).astype(jnp.bfloat16)
    k = seg(lo + ATT_WIDTH, ATT_WIDTH)
    store_heads(k_ref, k)
    kb_ref[...] = k.astype(jnp.bfloat16)
    v = seg(lo + 2 * ATT_WIDTH, ATT_WIDTH)
    store_heads(v_ref, v)
    if len(vb_ref.shape) == 2:
        vb_ref[...] = v.astype(jnp.bfloat16)
    else:
        vt = lax.dot_general(wvt_ref[...], xn, (((1,), (1,)), ((), ())),
                             preferred_element_type=jnp.float32).astype(jnp.bfloat16)
        ones_rows = (lax.broadcasted_iota(jnp.int32, (BF16_SUBLANES, vt.shape[1]), 0) == 0).astype(jnp.bfloat16)
        for h in range(ATT_HEADS):
            base = h * ATT_VT_ROWS
            vb_ref[0, base:base + ATT_V_DIM, :] = vt[h * ATT_V_DIM:(h + 1) * ATT_V_DIM, :]
            vb_ref[0, base + ATT_V_DIM:base + ATT_VT_ROWS, :] = ones_rows


def _in_proj(x, g, w, wvt, k_all, v_all, layer, tm, bk):
    n = x.shape[0]
    nt = n // tm
    row = lambda width: pl.BlockSpec((tm, width), lambda i: (i, 0))
    heads_rows = pl.BlockSpec((ATT_HEADS * tm, ATT_V_DIM), lambda i: (layer * nt + i, 0))
    untouched = pl.BlockSpec(memory_space=pl.ANY)
    all_shape = jax.ShapeDtypeStruct(k_all.shape, k_all.dtype)
    f32, bf16 = jnp.float32, jnp.bfloat16
    if bk is None:
        vb_spec, vb_shape = row(ATT_WIDTH), jax.ShapeDtypeStruct((n, ATT_WIDTH), bf16)
    else:
        per_block = bk // tm
        assert per_block * tm == bk
        vb_spec = pl.BlockSpec((1, ATT_HEADS * ATT_VT_ROWS, tm), lambda i: (i // per_block, 0, i % per_block))
        vb_shape = jax.ShapeDtypeStruct((n // bk, ATT_HEADS * ATT_VT_ROWS, bk), bf16)
    return pl.pallas_call(
        _in_proj_kernel,
        grid=(nt,),
        in_specs=[row(D_MODEL), _resident((1, D_MODEL)), _resident((D_MODEL, IN_COLS_PADDED)),
                  _resident((ATT_WIDTH, D_MODEL)), untouched, untouched],
        out_specs=[row(LRU_IN_W), row(SSD_IN_W), row(ATT_WIDTH), heads_rows, heads_rows,
                   row(ATT_WIDTH), vb_spec],
        out_shape=[jax.ShapeDtypeStruct((n, LRU_IN_W), f32), jax.ShapeDtypeStruct((n, SSD_IN_W), f32),
                   jax.ShapeDtypeStruct((n, ATT_WIDTH), bf16), all_shape, all_shape,
                   jax.ShapeDtypeStruct((n, ATT_WIDTH), bf16), vb_shape],
        input_output_aliases={4: 3, 5: 4},
        compiler_params=_params(_ARB),
        name="in_proj",
    )(x, g, w, wvt, k_all, v_all)


def _causal_conv(xp_ref, x, prev_ref, w_ref, b_ref, first):
    t = x.shape[0]
    keep = CONV_W - 1

    @pl.when(first)
    def _():
        xp_ref[CONV_PAD - keep:CONV_PAD, :] = prev_ref[0]

    @pl.when(jnp.logical_not(first))
    def _():
        xp_ref[CONV_PAD - keep:CONV_PAD, :] = xp_ref[CONV_PAD + t - keep:CONV_PAD + t, :]

    xp_ref[CONV_PAD:CONV_PAD + t, :] = x
    y = b_ref[...] + w_ref[CONV_W - 1:CONV_W, :] * x
    for j in range(CONV_W - 1):
        y = y + w_ref[j:j + 1, :] * xp_ref[CONV_PAD - keep + j:CONV_PAD - keep + j + t, :]
    return y


def _row_scan(step, vals, t):
    row = lax.broadcasted_iota(jnp.int32, (t, 1), 0)
    d = 1
    while d < t:
        shifted = tuple(pltpu.roll(v, d, 0) for v in vals)
        new = step(shifted, vals)
        vals = tuple(jnp.where(row >= d, n, v) for n, v in zip(new, vals))
        d *= 2
    return vals


def _lru_kernel(in_ref, conv0_ref, h0_ref, cw_ref, cb_ref, wg_ref, bg_ref, nsp_ref,
                y_ref, conv_out_ref, h_out_ref, xp_ref, h_ref):
    i = pl.program_id(1)
    first = i == 0
    t = in_ref.shape[0]

    @pl.when(first)
    def _():
        h_ref[...] = h0_ref[0]

    gate = in_ref[:, 0:LRU_WIDTH]
    x = _causal_conv(xp_ref, in_ref[:, LRU_WIDTH:LRU_IN_W], conv0_ref, cw_ref, cb_ref, first)
    pre = jnp.dot(x.astype(jnp.bfloat16), wg_ref[...], preferred_element_type=jnp.float32) + bg_ref[...]
    r = jax.nn.sigmoid(pre[:, 0:LRU_WIDTH])
    ig = jax.nn.sigmoid(pre[:, LRU_WIDTH:2 * LRU_WIDTH])
    log_a = r * nsp_ref[...]
    a = jnp.exp(log_a)
    b = jnp.sqrt(-jnp.tanh(log_a) * (a * a + 1.0)) * (ig * x)

    def step(shifted, cur):
        a_s, b_s = shifted
        a_c, b_c = cur
        return a_c * a_s, a_c * b_s + b_c

    a_cum, b_cum = _row_scan(step, (a, b), t)
    h = a_cum * h_ref[...] + b_cum
    h_ref[...] = h[t - 1:t, :]
    g = gate
    gelu = 0.5 * g * (1.0 + jnp.tanh(math.sqrt(2.0 / math.pi) * (g + 0.044715 * (g * g * g))))
    y_ref[...] = (gelu * h).astype(y_ref.dtype)

    @pl.when(i == pl.num_programs(1) - 1)
    def _():
        keep = CONV_W - 1
        conv_out_ref[0] = xp_ref[CONV_PAD + t - keep:CONV_PAD + t, :]
        h_out_ref[0] = h[t - 1:t, :]


def _lru(lru_in, conv0, h0, cw, cb, wg, bg, nsp, bsz, t):
    n = lru_in.shape[0]
    nt = n // bsz // t
    keep = CONV_W - 1
    per_b = lambda shape: pl.BlockSpec((1,) + shape, lambda b, i: (b,) + (0,) * len(shape))
    f32 = jnp.float32
    return pl.pallas_call(
        _lru_kernel,
        grid=(bsz, nt),
        in_specs=[pl.BlockSpec((t, LRU_IN_W), lambda b, i: (b * nt + i, 0)),
                  per_b((keep, LRU_WIDTH)), per_b((1, LRU_WIDTH)),
                  _resident((CONV_W, LRU_WIDTH)), _resident((1, LRU_WIDTH)),
                  _resident((LRU_WIDTH, 2 * LRU_WIDTH)), _resident((1, 2 * LRU_WIDTH)),
                  _resident((1, LRU_WIDTH))],
        out_specs=[pl.BlockSpec((t, LRU_WIDTH), lambda b, i: (b * nt + i, 0)),
                   per_b((keep, LRU_WIDTH)), per_b((1, LRU_WIDTH))],
        out_shape=[jax.ShapeDtypeStruct((n, LRU_WIDTH), jnp.bfloat16),
                   jax.ShapeDtypeStruct((bsz, keep, LRU_WIDTH), f32),
                   jax.ShapeDtypeStruct((bsz, 1, LRU_WIDTH), f32)],
        scratch_shapes=[pltpu.VMEM((CONV_PAD + t, LRU_WIDTH), f32), pltpu.VMEM((1, LRU_WIDTH), f32)],
        compiler_params=_params(_ARB, _ARB),
        name="lru",
    )(lru_in, conv0, h0, cw, cb, wg, bg, nsp)


def _ssd_kernel(in_ref, conv0_ref, h0_ref, cw_ref, cb_ref, dtb_ref, aneg_ref, dskip_ref, ng_ref,
                y_ref, conv_out_ref, h_out_ref, xp_ref, h_ref):
    i = pl.program_id(1)
    first = i == 0
    t = in_ref.shape[0]
    bf16 = jnp.bfloat16
    gw = 2 * SSD_HEAD_DIM

    @pl.when(first)
    def _():
        h_ref[...] = h0_ref[0]

    z = in_ref[:, 0:SSD_WIDTH]
    xc = _causal_conv(xp_ref, in_ref[:, SSD_WIDTH:SSD_WIDTH + SSD_CONV_CH], conv0_ref, cw_ref, cb_ref, first)
    xc = xc * jax.nn.sigmoid(xc)
    dt = jax.nn.softplus(in_ref[:, SSD_WIDTH + SSD_CONV_CH:SSD_IN_W] + dtb_ref[...])
    (cs,) = _row_scan(lambda s, c: (s[0] + c[0],), (dt * aneg_ref[...],), t)
    cs_t = cs.T
    cs_last = cs[t - 1:t, :]
    tail = jnp.exp(cs_last - cs)
    grow = jnp.exp(cs)
    chunk_decay = jnp.exp(cs_last)

    lane = lax.broadcasted_iota(jnp.int32, (t, gw), 1)
    low = lane < SSD_HEAD_DIM
    srow = lax.broadcasted_iota(jnp.int32, (gw, 1), 0) < SSD_HEAD_DIM
    causal = lax.broadcasted_iota(jnp.int32, (t, t), 0) >= lax.broadcasted_iota(jnp.int32, (t, t), 1)

    def pick(arr, g):
        c0 = jnp.broadcast_to(arr[:, 2 * g:2 * g + 1], (t, gw))
        c1 = jnp.broadcast_to(arr[:, 2 * g + 1:2 * g + 2], (t, gw))
        return jnp.where(low, c0, c1)

    ys = []
    for g in range(SSD_GROUPS):
        xs = xc[:, g * gw:(g + 1) * gw]
        bm = xc[:, SSD_WIDTH + g * SSD_STATE:SSD_WIDTH + (g + 1) * SSD_STATE].astype(bf16)
        cm = xc[:, SSD_WIDTH + (SSD_GROUPS + g) * SSD_STATE:SSD_WIDTH + (SSD_GROUPS + g + 1) * SSD_STATE].astype(bf16)
        xdt = xs * pick(dt, g)
        xdt_b = xdt.astype(bf16)
        cb = lax.dot_general(cm, bm, (((1,), (1,)), ((), ())), preferred_element_type=jnp.float32)
        y_heads = []
        for hh in range(2):
            hd = 2 * g + hh
            seg = jnp.broadcast_to(cs[:, hd:hd + 1], (t, t)) - cs_t[hd:hd + 1, :]
            scores = (cb * jnp.exp(jnp.where(causal, seg, NEG_BIG))).astype(bf16)
            y_heads.append(jnp.dot(scores, xdt_b, preferred_element_type=jnp.float32))
        y_intra = jnp.where(low, y_heads[0], y_heads[1])
        h_g = h_ref[g]
        y_inter = lax.dot_general(cm, h_g.astype(bf16), (((1,), (1,)), ((), ())),
                                  preferred_element_type=jnp.float32) * pick(grow, g)
        s_local = lax.dot_general((xdt * pick(tail, g)).astype(bf16), bm, (((0,), (0,)), ((), ())),
                                  preferred_element_type=jnp.float32)
        dec = jnp.where(srow, chunk_decay[:, 2 * g:2 * g + 1], chunk_decay[:, 2 * g + 1:2 * g + 2])
        h_ref[g] = dec * h_g + s_local
        y = y_intra + y_inter + dskip_ref[:, g * gw:(g + 1) * gw] * xs
        zg = z[:, g * gw:(g + 1) * gw]
        y = y * (zg * jax.nn.sigmoid(zg))
        ys.append(_rms(y, ng_ref[:, g * gw:(g + 1) * gw]))
    y_ref[...] = jnp.concatenate(ys, axis=1).astype(y_ref.dtype)

    @pl.when(i == pl.num_programs(1) - 1)
    def _():
        keep = CONV_W - 1
        conv_out_ref[0] = xp_ref[CONV_PAD + t - keep:CONV_PAD + t, :]
        h_out_ref[0] = h_ref[...]


def _ssd(ssd_in, conv0, h0, cw, cb, dtb, aneg, dskip, ng, bsz, t):
    n = ssd_in.shape[0]
    nt = n // bsz // t
    keep = CONV_W - 1
    gw = 2 * SSD_HEAD_DIM
    per_b = lambda shape: pl.BlockSpec((1,) + shape, lambda b, i: (b,) + (0,) * len(shape))
    f32 = jnp.float32
    return pl.pallas_call(
        _ssd_kernel,
        grid=(bsz, nt),
        in_specs=[pl.BlockSpec((t, SSD_IN_W), lambda b, i: (b * nt + i, 0)),
                  per_b((keep, SSD_CONV_CH)), per_b((SSD_GROUPS, gw, SSD_STATE)),
                  _resident((CONV_W, SSD_CONV_CH)), _resident((1, SSD_CONV_CH)),
                  _resident((1, DT_PAD)), _resident((1, DT_PAD)),
                  _resident((1, SSD_WIDTH)), _resident((1, SSD_WIDTH))],
        out_specs=[pl.BlockSpec((t, SSD_WIDTH), lambda b, i: (b * nt + i, 0)),
                   per_b((keep, SSD_CONV_CH)), per_b((SSD_GROUPS, gw, SSD_STATE))],
        out_shape=[jax.ShapeDtypeStruct((n, SSD_WIDTH), jnp.bfloat16),
                   jax.ShapeDtypeStruct((bsz, keep, SSD_CONV_CH), f32),
                   jax.ShapeDtypeStruct((bsz, SSD_GROUPS, gw, SSD_STATE), f32)],
        scratch_shapes=[pltpu.VMEM((CONV_PAD + t, SSD_CONV_CH), f32),
                        pltpu.VMEM((SSD_GROUPS, gw, SSD_STATE), f32)],
        compiler_params=_params(_ARB, _ARB),
        name="ssd",
    )(ssd_in, conv0, h0, cw, cb, dtb, aneg, dskip, ng)


def _stack_q(q):
    lane = lax.broadcasted_iota(jnp.int32, q.shape, 1)
    zero = jnp.zeros_like(q)
    return jnp.concatenate([jnp.where(lane < ATT_HEAD_DIM, q, zero),
                            jnp.where(lane < ATT_HEAD_DIM, zero, q)], axis=0)


def _diff_lambda(lv, lam_init):
    e1 = jnp.exp(jnp.sum(lv[0:1, :] * lv[1:2, :], axis=1, keepdims=True))
    e2 = jnp.exp(jnp.sum(lv[2:3, :] * lv[3:4, :], axis=1, keepdims=True))
    return e1 - e2 + lam_init


def _attn_finish(acc, l, lam, g, lam_init, bq):
    o = acc[0:bq] / l[0:bq] - lam * (acc[bq:2 * bq] / l[bq:2 * bq])
    return _rms(o, g) * (1.0 - lam_init)


def _attn_prompt_kernel(q_ref, k_ref, vt_ref, lv_ref, g_ref, o_ref, acc0_ref, acc1_ref, s_ref,
                        *, lam_init, bq):
    i = pl.program_id(2)
    unroll = 2
    accs = (acc0_ref, acc1_ref)
    qs = _stack_q(q_ref[...])
    qs = (qs[0:bq], qs[bq:2 * bq])
    for a in accs:
        a[...] = jnp.zeros(a.shape, jnp.float32)

    def scores(j):
        k = k_ref[pl.ds(pl.multiple_of(j * bq, bq), bq), :]
        return tuple(lax.dot_general(k, qs[mp], (((1,), (1,)), ((), ())),
                                     preferred_element_type=jnp.float32) for mp in range(2))

    def consume(j, slot, mask, stats):
        vt = vt_ref[j]
        out = []
        for mp in range(2):
            m_old = stats[mp]
            s = s_ref[2 * slot + mp]
            if mask is not None:
                s = jnp.where(mask, s, NEG_BIG)
            m_new = jnp.maximum(m_old, jnp.max(s, axis=0, keepdims=True))
            p = jnp.exp2(s - m_new)
            alpha = jnp.exp2(m_old - m_new)
            accs[mp][...] = alpha * accs[mp][...] + jnp.dot(vt, p.astype(jnp.bfloat16),
                                                            preferred_element_type=jnp.float32)
            out.append(m_new)
        return tuple(out)

    def store_scores(s, slot):
        for mp in range(2):
            s_ref[2 * slot + mp] = s[mp]

    store_scores(scores(0), 0)

    def step(j, slot, stats):
        s_next = scores(j + 1)
        stats = consume(j, slot, None, stats)
        store_scores(s_next, 1 - slot)
        return stats

    def group(jj, stats):
        for u in range(unroll):
            stats = step(unroll * jj + u, u % 2, stats)
        return stats

    init = (jnp.full((1, bq), NEG_BIG, jnp.float32),) * 2
    groups = i // unroll
    stats = lax.fori_loop(0, groups, group, init)
    stats = lax.fori_loop(unroll * groups, i, lambda j, st: step(j, j % 2, st), stats)
    kpos = lax.broadcasted_iota(jnp.int32, (bq, bq), 0)
    qpos = lax.broadcasted_iota(jnp.int32, (bq, bq), 1)
    consume(i, i % 2, (kpos // CHUNK) <= (qpos // CHUNK), stats)
    lam = _diff_lambda(lv_ref[...], lam_init)
    d = ATT_V_DIM
    o = (acc0_ref[0:d, :] / acc0_ref[d:d + 1, :]
         - lam * (acc1_ref[0:d, :] / acc1_ref[d:d + 1, :]))
    o = o * lax.rsqrt(jnp.mean(o * o, axis=0, keepdims=True) + EPS) * g_ref[...] * (1.0 - lam_init)
    o_ref[...] = o.T.astype(o_ref.dtype)


def _attn_prompt(q, kb, vt, lv, g, lam_init, bsz, bq):
    n = q.shape[0]
    seq = n // bsz
    nq = seq // bq
    assert bq % CHUNK == 0 and seq % bq == 0 and vt.shape == (n // bq, ATT_HEADS * ATT_VT_ROWS, bq)
    f32 = jnp.float32
    return pl.pallas_call(
        functools.partial(_attn_prompt_kernel, lam_init=lam_init, bq=bq),
        grid=(bsz, ATT_HEADS, nq),
        in_specs=[pl.BlockSpec((bq, ATT_V_DIM), lambda b, h, i: (b * nq + i, h)),
                  pl.BlockSpec((seq, ATT_V_DIM), lambda b, h, i: (b, h)),
                  pl.BlockSpec((nq, ATT_VT_ROWS, bq), lambda b, h, i: (b, h, 0)),
                  pl.BlockSpec((4, ATT_HEAD_DIM), lambda b, h, i: (0, 0)),
                  pl.BlockSpec((ATT_V_DIM, 1), lambda b, h, i: (0, 0))],
        out_specs=pl.BlockSpec((bq, ATT_V_DIM), lambda b, h, i: (b * nq + i, h)),
        out_shape=jax.ShapeDtypeStruct((n, ATT_WIDTH), jnp.bfloat16),
        scratch_shapes=[pltpu.VMEM((ATT_VT_ROWS, bq), f32), pltpu.VMEM((ATT_VT_ROWS, bq), f32),
                        pltpu.VMEM((4, bq, bq), f32)],
        compiler_params=_params(_ARB, _ARB, _ARB),
        name="attn_prompt",
    )(q, kb, vt, lv, g)


def _attn_sample_kernel(q_ref, kn_ref, vn_ref, kp_ref, vp_ref, lv_ref, g_ref, o_ref, *, lam_init, lq):
    lam = _diff_lambda(lv_ref[...], lam_init)
    bf16 = jnp.bfloat16
    nt = (((1,), (1,)), ((), ()))
    past_len = kp_ref.shape[0] // ATT_HEADS
    outs = []
    for h in range(ATT_HEADS):
        sl = slice(h * ATT_V_DIM, (h + 1) * ATT_V_DIM)
        qs = _stack_q(q_ref[:, sl])
        kp = kp_ref[pl.ds(h, past_len, stride=ATT_HEADS), :].astype(bf16)
        vp = vp_ref[pl.ds(h, past_len, stride=ATT_HEADS), :].astype(bf16)
        s_p = lax.dot_general(qs, kp, nt, preferred_element_type=jnp.float32)
        s_n = lax.dot_general(qs, kn_ref[:, sl], nt, preferred_element_type=jnp.float32)
        m = jnp.maximum(jnp.max(s_p, axis=1, keepdims=True), jnp.max(s_n, axis=1, keepdims=True))
        p_p = jnp.exp2(s_p - m)
        p_n = jnp.exp2(s_n - m)
        l = jnp.sum(p_p, axis=1, keepdims=True) + jnp.sum(p_n, axis=1, keepdims=True)
        acc = (jnp.dot(p_p.astype(bf16), vp, preferred_element_type=jnp.float32)
               + jnp.dot(p_n.astype(bf16), vn_ref[:, sl], preferred_element_type=jnp.float32))
        outs.append(_attn_finish(acc, l, lam, g_ref[...], lam_init, lq))
    o_ref[...] = jnp.concatenate(outs, axis=1).astype(o_ref.dtype)


def _attn_sample(q, kb, vb, past_k, past_v, layer, lv, g, lam_init, bsz, lq, past_len):
    assert past_len % CHUNK == 0 and lq <= CHUNK
    n = q.shape[0]
    step = lambda: pl.BlockSpec((lq, ATT_WIDTH), lambda b: (b, 0))
    past = lambda: pl.BlockSpec((past_len * ATT_HEADS, ATT_V_DIM), lambda b: (layer * bsz + b, 0))
    return pl.pallas_call(
        functools.partial(_attn_sample_kernel, lam_init=lam_init, lq=lq),
        grid=(bsz,),
        in_specs=[step(), step(), step(), past(), past(),
                  pl.BlockSpec((4, ATT_HEAD_DIM), lambda b: (0, 0)),
                  pl.BlockSpec((1, ATT_V_DIM), lambda b: (0, 0))],
        out_specs=step(),
        out_shape=jax.ShapeDtypeStruct((n, ATT_WIDTH), jnp.bfloat16),
        compiler_params=_params(_ARB),
        name="attn_sample",
    )(q, kb, vb, past_k, past_v, lv, g)


def _out_ffn_kernel(x_ref, yl_ref, ys_ref, ya_ref, wo_ref, gf_ref, wu_ref, wd_ref, gn_ref, o_ref,
                    *, ff_chunk, final_norm):
    f32 = jnp.float32
    mixed = jnp.concatenate([yl_ref[...], ys_ref[...], ya_ref[...]], axis=1)
    x = x_ref[...] + jnp.dot(mixed, wo_ref[...], preferred_element_type=f32)
    hn = _rms(x, gf_ref[...]).astype(jnp.bfloat16)
    for c in range(D_FF // ff_chunk):
        u = jnp.dot(hn, wu_ref[:, c * ff_chunk:(c + 1) * ff_chunk], preferred_element_type=f32)
        u = jnp.square(jnp.maximum(u, 0.0)).astype(jnp.bfloat16)
        x = x + jnp.dot(u, wd_ref[c * ff_chunk:(c + 1) * ff_chunk, :], preferred_element_type=f32)
    if final_norm:
        x = _rms(x, gn_ref[...])
    o_ref[...] = x


def _out_ffn(x, yl, ys, ya, wo, gf, wu, wd, gn, tm, final_norm, ff_chunk=1024):
    n = x.shape[0]
    row = lambda width: pl.BlockSpec((tm, width), lambda i: (i, 0))
    return pl.pallas_call(
        functools.partial(_out_ffn_kernel, ff_chunk=ff_chunk, final_norm=final_norm),
        grid=(n // tm,),
        in_specs=[row(D_MODEL), row(LRU_WIDTH), row(SSD_WIDTH), row(ATT_WIDTH),
                  _resident((D_MODEL, D_MODEL)), _resident((1, D_MODEL)),
                  _resident((D_MODEL, D_FF)), _resident((D_FF, D_MODEL)), _resident((1, D_MODEL))],
        out_specs=row(D_MODEL),
        out_shape=jax.ShapeDtypeStruct((n, D_MODEL), jnp.float32),
        compiler_params=_params(_ARB),
        name="out_ffn",
    )(x, yl, ys, ya, wo, gf, wu, wd, gn)


def _prep_w_in(w_in):
    o_dt = 2 * LRU_WIDTH + SSD_WIDTH + SSD_CONV_CH
    head = w_in[..., :o_dt]
    dt = jnp.pad(w_in[..., o_dt:o_dt + SSD_HEADS], ((0, 0), (0, 0), (0, DT_PAD - SSD_HEADS)))
    tail = w_in[..., o_dt + SSD_HEADS:]
    return jnp.concatenate([head, dt, tail], axis=-1).astype(jnp.bfloat16)


def _block_diag(w):
    eye = jnp.eye(LRU_BLOCKS, dtype=w.dtype)
    full = w[:, :, :, None, :] * eye[None, :, None, :, None]
    return full.reshape(w.shape[0], LRU_WIDTH, LRU_WIDTH)


def _pad_heads(v):
    return jnp.pad(v, ((0, 0), (0, DT_PAD - SSD_HEADS)))[:, None, :]


def _layer(x, states, kv_all, layer, p, lam_init, bsz, tiles, attn, vt_block, final_norm):
    tm, t_lru, t_ssd = tiles
    conv_lru0, h_lru0, conv_ssd0, h_ssd0 = states
    lru_in, ssd_in, q, k_all, v_all, kb, vb = _in_proj(x, p["norm_mix_g"], p["w_in"], p["w_vt"],
                                                       kv_all[0], kv_all[1], layer, tm, vt_block)
    y_lru, conv_lru, h_lru = _lru(lru_in, conv_lru0, h_lru0, p["lru_conv_w"], p["lru_conv_b"],
                                  p["lru_wg"], p["lru_bg"], p["lru_nsp"], bsz, t_lru)
    y_ssd, conv_ssd, h_ssd = _ssd(ssd_in, conv_ssd0, h_ssd0, p["ssd_conv_w"], p["ssd_conv_b"],
                                  p["ssd_dt_bias"], p["ssd_a_neg"], p["ssd_d"], p["ssd_norm_g"], bsz, t_ssd)
    y_att = attn(q, kb, vb)
    x = _out_ffn(x, y_lru, y_ssd, y_att, p["w_out"], p["norm_ffn_g"], p["w_up"], p["w_down"],
                 p["norm_f_g"], tm, final_norm)
    return x, (k_all, v_all), (conv_lru, h_lru, conv_ssd, h_ssd)


def _forward(x_prompt, x_sample, cache_att_k, cache_att_v, state_lru_conv, state_lru_h,
             state_ssd_conv, state_ssd_h, norm_mix_g, w_in, lru_conv_w, lru_conv_b,
             lru_wa, lru_ba, lru_wx, lru_bx, lru_lambda, ssd_conv_w, ssd_conv_b,
             ssd_dt_bias, ssd_a_log, ssd_d, ssd_norm_g, att_lambda, att_subln_g,
             w_out, norm_ffn_g, w_up, w_down, norm_f_g, *, tiles_p, tiles_s, bq):
    f32, bf16 = jnp.float32, jnp.bfloat16
    depth = w_in.shape[0]
    bp, seq, _ = x_prompt.shape
    bs, lq, _ = x_sample.shape
    past_len = cache_att_k.shape[2]
    gw = 2 * SSD_HEAD_DIM

    stacked = {
        "norm_mix_g": norm_mix_g[:, None, :],
        "w_in": _prep_w_in(w_in),
        "w_vt": jnp.swapaxes(w_in[..., -ATT_WIDTH:], 1, 2).astype(bf16),
        "lru_conv_w": lru_conv_w, "lru_conv_b": lru_conv_b[:, None, :],
        "lru_wg": jnp.concatenate([_block_diag(lru_wa), _block_diag(lru_wx)], axis=-1).astype(bf16),
        "lru_bg": jnp.concatenate([lru_ba, lru_bx], axis=-1)[:, None, :],
        "lru_nsp": (-LRU_C * jax.nn.softplus(-lru_lambda.astype(f32)))[:, None, :],
        "ssd_conv_w": ssd_conv_w, "ssd_conv_b": ssd_conv_b[:, None, :],
        "ssd_dt_bias": _pad_heads(ssd_dt_bias),
        "ssd_a_neg": _pad_heads(-jnp.exp(ssd_a_log.astype(f32))),
        "ssd_d": jnp.repeat(ssd_d, SSD_HEAD_DIM, axis=-1)[:, None, :],
        "ssd_norm_g": ssd_norm_g[:, None, :],
        "att_lambda": att_lambda, "att_subln_g": att_subln_g[:, None, :],
        "att_subln_g_col": att_subln_g[:, :, None],
        "w_out": w_out.astype(bf16), "norm_ffn_g": norm_ffn_g[:, None, :],
        "w_up": w_up.astype(bf16), "w_down": w_down.astype(bf16),
    }
    gn = norm_f_g[None, :]

    xp = x_prompt.reshape(bp * seq, D_MODEL)
    xs = x_sample.reshape(bs * lq, D_MODEL)
    past_k = cache_att_k.reshape(depth * bs * past_len * ATT_HEADS, ATT_V_DIM)
    past_v = cache_att_v.reshape(depth * bs * past_len * ATT_HEADS, ATT_V_DIM)
    zero_states = (jnp.zeros((bp, CONV_W - 1, LRU_WIDTH), f32), jnp.zeros((bp, 1, LRU_WIDTH), f32),
                   jnp.zeros((bp, CONV_W - 1, SSD_CONV_CH), f32),
                   jnp.zeros((bp, SSD_GROUPS, gw, SSD_STATE), f32))
    kv_p = (jnp.zeros((depth * bp * seq * ATT_HEADS, ATT_V_DIM), f32),) * 2
    kv_s = (jnp.zeros((depth * bs * lq * ATT_HEADS, ATT_V_DIM), f32),) * 2
    st_p, st_s = [], []
    for l in range(depth):
        p = {name: v[l] for name, v in stacked.items()}
        p["norm_f_g"] = gn
        lam_init = 0.8 - 0.6 * math.exp(-0.3 * l)
        last = l == depth - 1
        attn_p = functools.partial(_attn_prompt, lv=p["att_lambda"], g=p["att_subln_g_col"],
                                   lam_init=lam_init, bsz=bp, bq=bq)
        xp, kv_p, sp = _layer(xp, zero_states, kv_p, l, p, lam_init, bp, tiles_p, attn_p, bq, last)
        states_s = (state_lru_conv[l], state_lru_h[l][:, None, :], state_ssd_conv[l],
                    state_ssd_h[l].reshape(bs, SSD_GROUPS, gw, SSD_STATE))
        attn_s = functools.partial(_attn_sample, past_k=past_k, past_v=past_v, layer=l,
                                   lv=p["att_lambda"], g=p["att_subln_g"], lam_init=lam_init,
                                   bsz=bs, lq=lq, past_len=past_len)
        xs, kv_s, ss = _layer(xs, states_s, kv_s, l, p, lam_init, bs, tiles_s, attn_s, None, last)
        st_p.append(sp)
        st_s.append(ss)

    def collect(kv, sts, bsz, length):
        stack = lambda i: jnp.stack([s[i] for s in sts], axis=0)
        return (kv[0].reshape(depth, bsz, length, ATT_HEADS, ATT_V_DIM),
                kv[1].reshape(depth, bsz, length, ATT_HEADS, ATT_V_DIM),
                stack(0), stack(1).reshape(depth, bsz, LRU_WIDTH), stack(2),
                stack(3).reshape(depth, bsz, SSD_HEADS, SSD_HEAD_DIM, SSD_STATE))

    return ((xp.reshape(bp, seq, D_MODEL), xs.reshape(bs, lq, D_MODEL))
            + collect(kv_p, st_p, bp, seq) + collect(kv_s, st_s, bs, lq))


def kernel(x_prompt, x_sample, cache_att_k, cache_att_v, state_lru_conv, state_lru_h, state_ssd_conv, state_ssd_h, norm_mix_g, w_in, lru_conv_w, lru_conv_b, lru_wa, lru_ba, lru_wx, lru_bx, lru_lambda, ssd_conv_w, ssd_conv_b, ssd_dt_bias, ssd_a_log, ssd_d, ssd_norm_g, att_lambda, att_subln_g, w_out, norm_ffn_g, w_up, w_down, norm_f_g):
    lq = x_sample.shape[1]
    n_s = x_sample.shape[0] * lq
    return _forward(x_prompt, x_sample, cache_att_k, cache_att_v, state_lru_conv, state_lru_h,
                    state_ssd_conv, state_ssd_h, norm_mix_g, w_in, lru_conv_w, lru_conv_b,
                    lru_wa, lru_ba, lru_wx, lru_bx, lru_lambda, ssd_conv_w, ssd_conv_b,
                    ssd_dt_bias, ssd_a_log, ssd_d, ssd_norm_g, att_lambda, att_subln_g,
                    w_out, norm_ffn_g, w_up, w_down, norm_f_g,
                    tiles_p=(512, 512, 256), tiles_s=(n_s, lq, lq), bq=1024)
```

```python
import functools
import math

import jax
import jax.numpy as jnp
from jax import lax
from jax.experimental import pallas as pl
from jax.experimental.pallas import tpu as pltpu

D_MODEL = 1024
DEPTH = 4
CHUNK = 64
CONV_W = 4
LRU_WIDTH = 256
LRU_BLOCKS = 4
LRU_BLOCK = LRU_WIDTH // LRU_BLOCKS
LRU_C = 8.0
SSD_WIDTH = 256
SSD_HEAD_DIM = 64
SSD_HEADS = 4
SSD_GROUPS = 2
SSD_STATE = 128
SSD_CONV_CH = SSD_WIDTH + 2 * SSD_GROUPS * SSD_STATE
ATT_WIDTH = 512
ATT_HEADS = 4
ATT_V_DIM = 128
ATT_HEAD_DIM = 64
ATT_SCALE = ATT_HEAD_DIM ** -0.5
LOG2_E = math.log2(math.e)
D_FF = 4 * D_MODEL
EPS = 1e-6

LANES = 128
BF16_SUBLANES = 16
ATT_VT_ROWS = ATT_V_DIM + BF16_SUBLANES
CONV_PAD = 8
DT_PAD = LANES
LRU_IN_W = 2 * LRU_WIDTH
SSD_IN_W = SSD_WIDTH + SSD_CONV_CH + DT_PAD
IN_COLS_PADDED = LRU_IN_W + SSD_IN_W + 3 * ATT_WIDTH
NEG_BIG = -1e30
VMEM_LIMIT = 56 * 1024 * 1024

_ARB = "arbitrary"


def _params(*sem):
    return pltpu.CompilerParams(dimension_semantics=sem, vmem_limit_bytes=VMEM_LIMIT)


def _resident(shape):
    nd = len(shape)
    return pl.BlockSpec(shape, lambda *_: (0,) * nd, pipeline_mode=pl.Buffered(1))


def _rms(x, g):
    return x * lax.rsqrt(jnp.mean(x * x, axis=-1, keepdims=True) + EPS) * g


def _in_proj_kernel(x_ref, g_ref, w_ref, k_all_ref, v_all_ref,
                    lru_ref, ssd_ref, q_ref, k_ref, v_ref, kb_ref, vb_ref):
    del k_all_ref, v_all_ref
    tm = x_ref.shape[0]
    xn = _rms(x_ref[...], g_ref[...]).astype(jnp.bfloat16)

    def store_heads(ref, val):
        for h in range(ATT_HEADS):
            ref[pl.ds(h, tm, stride=ATT_HEADS), :] = val[:, h * ATT_V_DIM:(h + 1) * ATT_V_DIM]


    def seg(lo, width):
        return jnp.dot(xn, w_ref[:, lo:lo + width], preferred_element_type=jnp.float32)

    lru_ref[...] = seg(0, LRU_IN_W)
    ssd_ref[...] = seg(LRU_IN_W, SSD_IN_W)
    lo = LRU_IN_W + SSD_IN_W
    q_ref[...] = (seg(lo, ATT_WIDTH) * (ATT_SCALE * LOG2_E)).astype(jnp.bfloat16)
    k = seg(lo + ATT_WIDTH, ATT_WIDTH)
    store_heads(k_ref, k)
    kb_ref[...] = k.astype(jnp.bfloat16)
    v = seg(lo + 2 * ATT_WIDTH, ATT_WIDTH)
    store_heads(v_ref, v)
    if len(vb_ref.shape) == 2:
        vb_ref[...] = v.astype(jnp.bfloat16)
    else:
        ones_rows = (lax.broadcasted_iota(jnp.int32, (BF16_SUBLANES, tm), 0) == 0).astype(jnp.bfloat16)
        for h in range(ATT_HEADS):
            base = h * ATT_VT_ROWS
            vb_ref[0, base:base + ATT_V_DIM, :] = v[:, h * ATT_V_DIM:(h + 1) * ATT_V_DIM].T.astype(jnp.bfloat16)
            vb_ref[0, base + ATT_V_DIM:base + ATT_VT_ROWS, :] = ones_rows


def _in_proj(x, g, w, k_all, v_all, layer, tm, bk):
    n = x.shape[0]
    nt = n // tm
    row = lambda width: pl.BlockSpec((tm, width), lambda i: (i, 0))
    heads_rows = pl.BlockSpec((ATT_HEADS * tm, ATT_V_DIM), lambda i: (layer * nt + i, 0))
    untouched = pl.BlockSpec(memory_space=pl.ANY)
    all_shape = jax.ShapeDtypeStruct(k_all.shape, k_all.dtype)
    f32, bf16 = jnp.float32, jnp.bfloat16
    if bk is None:
        vb_spec, vb_shape = row(ATT_WIDTH), jax.ShapeDtypeStruct((n, ATT_WIDTH), bf16)
    else:
        per_block = bk // tm
        assert per_block * tm == bk
        vb_spec = pl.BlockSpec((1, ATT_HEADS * ATT_VT_ROWS, tm), lambda i: (i // per_block, 0, i % per_block))
        vb_shape = jax.ShapeDtypeStruct((n // bk, ATT_HEADS * ATT_VT_ROWS, bk), bf16)
    return pl.pallas_call(
        _in_proj_kernel,
        grid=(nt,),
        in_specs=[row(D_MODEL), _resident((1, D_MODEL)), _resident((D_MODEL, IN_COLS_PADDED)),
                  untouched, untouched],
        out_specs=[row(LRU_IN_W), row(SSD_IN_W), row(ATT_WIDTH), heads_rows, heads_rows,
                   row(ATT_WIDTH), vb_spec],
        out_shape=[jax.ShapeDtypeStruct((n, LRU_IN_W), f32), jax.ShapeDtypeStruct((n, SSD_IN_W), f32),
                   jax.ShapeDtypeStruct((n, ATT_WIDTH), bf16), all_shape, all_shape,
                   jax.ShapeDtypeStruct((n, ATT_WIDTH), bf16), vb_shape],
        input_output_aliases={3: 3, 4: 4},
        compiler_params=_params(_ARB),
        name="in_proj",
    )(x, g, w, k_all, v_all)


def _state_init(first, xp_ref, conv0_ref, h_ref, h0_ref):
    t = xp_ref.shape[0] - CONV_PAD
    keep = CONV_W - 1

    @pl.when(first)
    def _():
        xp_ref[CONV_PAD - keep:CONV_PAD, :] = conv0_ref[0]
        h_ref[...] = h0_ref[0]

    @pl.when(jnp.logical_not(first))
    def _():
        xp_ref[CONV_PAD - keep:CONV_PAD, :] = xp_ref[CONV_PAD + t - keep:CONV_PAD + t, :]


def _state_emit(last, xp_ref, conv_out_ref, h_ref, h_out_ref):
    t = xp_ref.shape[0] - CONV_PAD
    keep = CONV_W - 1

    @pl.when(last)
    def _():
        conv_out_ref[0] = xp_ref[CONV_PAD + t - keep:CONV_PAD + t, :]
        h_out_ref[0] = h_ref[...]


def _causal_conv(xp_ref, x, w_ref, b_ref):
    t = x.shape[0]
    keep = CONV_W - 1
    xp_ref[CONV_PAD:CONV_PAD + t, :] = x
    y = b_ref[...] + w_ref[CONV_W - 1:CONV_W, :] * x
    for j in range(CONV_W - 1):
        y = y + w_ref[j:j + 1, :] * xp_ref[CONV_PAD - keep + j:CONV_PAD - keep + j + t, :]
    return y


def _row_scan(step, vals, t):
    row = lax.broadcasted_iota(jnp.int32, (t, 1), 0)
    d = 1
    while d < t:
        shifted = tuple(pltpu.roll(v, d, 0) for v in vals)
        new = step(shifted, vals)
        vals = tuple(jnp.where(row >= d, n, v) for n, v in zip(new, vals))
        d *= 2
    return vals


def _lru_tile(in_ref, cw_ref, cb_ref, wg_ref, bg_ref, nsp_ref, xp_ref, h_ref):
    t = in_ref.shape[0]
    gate = in_ref[:, 0:LRU_WIDTH]
    x = _causal_conv(xp_ref, in_ref[:, LRU_WIDTH:LRU_IN_W], cw_ref, cb_ref)
    pre = jnp.dot(x.astype(jnp.bfloat16), wg_ref[...], preferred_element_type=jnp.float32) + bg_ref[...]
    r = jax.nn.sigmoid(pre[:, 0:LRU_WIDTH])
    ig = jax.nn.sigmoid(pre[:, LRU_WIDTH:2 * LRU_WIDTH])
    log_a = r * nsp_ref[...]
    a = jnp.exp(log_a)
    b = jnp.sqrt(-jnp.tanh(log_a) * (a * a + 1.0)) * (ig * x)

    def step(shifted, cur):
        a_s, b_s = shifted
        a_c, b_c = cur
        return a_c * a_s, a_c * b_s + b_c

    a_cum, b_cum = _row_scan(step, (a, b), t)
    h = a_cum * h_ref[...] + b_cum
    h_ref[...] = h[t - 1:t, :]
    g = gate
    gelu = 0.5 * g * (1.0 + jnp.tanh(math.sqrt(2.0 / math.pi) * (g + 0.044715 * (g * g * g))))
    return gelu * h


def _lru_kernel(in_ref, conv0_ref, h0_ref, cw_ref, cb_ref, wg_ref, bg_ref, nsp_ref,
                y_ref, conv_out_ref, h_out_ref, xp_ref, h_ref):
    i = pl.program_id(1)
    _state_init(i == 0, xp_ref, conv0_ref, h_ref, h0_ref)
    y = _lru_tile(in_ref, cw_ref, cb_ref, wg_ref, bg_ref, nsp_ref, xp_ref, h_ref)
    y_ref[...] = y.astype(y_ref.dtype)
    _state_emit(i == pl.num_programs(1) - 1, xp_ref, conv_out_ref, h_ref, h_out_ref)


def _lru(lru_in, conv0, h0, cw, cb, wg, bg, nsp, bsz, t):
    n = lru_in.shape[0]
    nt = n // bsz // t
    keep = CONV_W - 1
    per_b = lambda shape: pl.BlockSpec((1,) + shape, lambda b, i: (b,) + (0,) * len(shape))
    f32 = jnp.float32
    return pl.pallas_call(
        _lru_kernel,
        grid=(bsz, nt),
        in_specs=[pl.BlockSpec((t, LRU_IN_W), lambda b, i: (b * nt + i, 0)),
                  per_b((keep, LRU_WIDTH)), per_b((1, LRU_WIDTH)),
                  _resident((CONV_W, LRU_WIDTH)), _resident((1, LRU_WIDTH)),
                  _resident((LRU_WIDTH, 2 * LRU_WIDTH)), _resident((1, 2 * LRU_WIDTH)),
                  _resident((1, LRU_WIDTH))],
        out_specs=[pl.BlockSpec((t, LRU_WIDTH), lambda b, i: (b * nt + i, 0)),
                   per_b((keep, LRU_WIDTH)), per_b((1, LRU_WIDTH))],
        out_shape=[jax.ShapeDtypeStruct((n, LRU_WIDTH), jnp.bfloat16),
                   jax.ShapeDtypeStruct((bsz, keep, LRU_WIDTH), f32),
                   jax.ShapeDtypeStruct((bsz, 1, LRU_WIDTH), f32)],
        scratch_shapes=[pltpu.VMEM((CONV_PAD + t, LRU_WIDTH), f32), pltpu.VMEM((1, LRU_WIDTH), f32)],
        compiler_params=_params(_ARB, _ARB),
        name="lru",
    )(lru_in, conv0, h0, cw, cb, wg, bg, nsp)


def _ssd_tile(in_ref, cw_ref, cb_ref, dtb_ref, aneg_ref, dskip_ref, ng_ref, xp_ref, h_ref, chunk):
    out = None
    for stage in _ssd_stages(in_ref, cw_ref, cb_ref, dtb_ref, aneg_ref, dskip_ref, ng_ref, xp_ref, h_ref, chunk):
        out = stage()
    return out


def _ssd_stages(in_ref, cw_ref, cb_ref, dtb_ref, aneg_ref, dskip_ref, ng_ref, xp_ref, h_ref, chunk):
    t = in_ref.shape[0]
    st = {}

    def prep():
        xc = _causal_conv(xp_ref, in_ref[:, SSD_WIDTH:SSD_WIDTH + SSD_CONV_CH], cw_ref, cb_ref)
        st["xc"] = xc * jax.nn.sigmoid(xc)
        st["dt"] = jax.nn.softplus(in_ref[:, SSD_WIDTH + SSD_CONV_CH:SSD_IN_W] + dtb_ref[...])
        st["h"] = [h_ref[g] for g in range(SSD_GROUPS)]
        st["ys"] = []

    def scan(c):
        rows = slice(c * chunk, (c + 1) * chunk)
        y, st["h"] = _ssd_chunk(in_ref[rows, 0:SSD_WIDTH], st["xc"][rows], st["dt"][rows], st["h"],
                                aneg_ref, dskip_ref, ng_ref)
        st["ys"].append(y)
        if c < t // chunk - 1:
            return None
        for g in range(SSD_GROUPS):
            h_ref[g] = st["h"][g]
        ys = st["ys"]
        return ys[0] if len(ys) == 1 else jnp.concatenate(ys, axis=0)

    return [prep] + [functools.partial(scan, c) for c in range(t // chunk)]


def _ssd_chunk(z, xc, dt, h, aneg_ref, dskip_ref, ng_ref):
    t = z.shape[0]
    bf16 = jnp.bfloat16
    gw = 2 * SSD_HEAD_DIM
    (cs,) = _row_scan(lambda s, c: (s[0] + c[0],), (dt * aneg_ref[...],), t)
    cs_t = cs.T
    cs_last = cs[t - 1:t, :]
    tail = jnp.exp(cs_last - cs)
    grow = jnp.exp(cs)
    chunk_decay = jnp.exp(cs_last)

    lane = lax.broadcasted_iota(jnp.int32, (t, gw), 1)
    low = lane < SSD_HEAD_DIM
    srow = lax.broadcasted_iota(jnp.int32, (gw, 1), 0) < SSD_HEAD_DIM
    causal = lax.broadcasted_iota(jnp.int32, (t, t), 0) >= lax.broadcasted_iota(jnp.int32, (t, t), 1)

    def pick(arr, g):
        c0 = jnp.broadcast_to(arr[:, 2 * g:2 * g + 1], (t, gw))
        c1 = jnp.broadcast_to(arr[:, 2 * g + 1:2 * g + 2], (t, gw))
        return jnp.where(low, c0, c1)

    ys, h_new = [], []
    for g in range(SSD_GROUPS):
        xs = xc[:, g * gw:(g + 1) * gw]
        bm = xc[:, SSD_WIDTH + g * SSD_STATE:SSD_WIDTH + (g + 1) * SSD_STATE].astype(bf16)
        cm = xc[:, SSD_WIDTH + (SSD_GROUPS + g) * SSD_STATE:SSD_WIDTH + (SSD_GROUPS + g + 1) * SSD_STATE].astype(bf16)
        xdt = xs * pick(dt, g)
        xdt_b = xdt.astype(bf16)
        cb = lax.dot_general(cm, bm, (((1,), (1,)), ((), ())), preferred_element_type=jnp.float32)
        y_heads = []
        for hh in range(2):
            hd = 2 * g + hh
            seg = jnp.broadcast_to(cs[:, hd:hd + 1], (t, t)) - cs_t[hd:hd + 1, :]
            scores = (cb * jnp.exp(jnp.where(causal, seg, NEG_BIG))).astype(bf16)
            y_heads.append(jnp.dot(scores, xdt_b, preferred_element_type=jnp.float32))
        y_intra = jnp.where(low, y_heads[0], y_heads[1])
        h_g = h[g]
        y_inter = lax.dot_general(cm, h_g.astype(bf16), (((1,), (1,)), ((), ())),
                                  preferred_element_type=jnp.float32) * pick(grow, g)
        s_local = lax.dot_general((xdt * pick(tail, g)).astype(bf16), bm, (((0,), (0,)), ((), ())),
                                  preferred_element_type=jnp.float32)
        dec = jnp.where(srow, chunk_decay[:, 2 * g:2 * g + 1], chunk_decay[:, 2 * g + 1:2 * g + 2])
        h_new.append(dec * h_g + s_local)
        y = y_intra + y_inter + dskip_ref[:, g * gw:(g + 1) * gw] * xs
        zg = z[:, g * gw:(g + 1) * gw]
        y = y * (zg * jax.nn.sigmoid(zg))
        ys.append(_rms(y, ng_ref[:, g * gw:(g + 1) * gw]))
    return jnp.concatenate(ys, axis=1), h_new


def _ssd_kernel(in_ref, conv0_ref, h0_ref, cw_ref, cb_ref, dtb_ref, aneg_ref, dskip_ref, ng_ref,
                y_ref, conv_out_ref, h_out_ref, xp_ref, h_ref):
    i = pl.program_id(1)
    _state_init(i == 0, xp_ref, conv0_ref, h_ref, h0_ref)
    y = _ssd_tile(in_ref, cw_ref, cb_ref, dtb_ref, aneg_ref, dskip_ref, ng_ref, xp_ref, h_ref,
                  in_ref.shape[0])
    y_ref[...] = y.astype(y_ref.dtype)
    _state_emit(i == pl.num_programs(1) - 1, xp_ref, conv_out_ref, h_ref, h_out_ref)


def _ssd(ssd_in, conv0, h0, cw, cb, dtb, aneg, dskip, ng, bsz, t):
    n = ssd_in.shape[0]
    nt = n // bsz // t
    keep = CONV_W - 1
    gw = 2 * SSD_HEAD_DIM
    per_b = lambda shape: pl.BlockSpec((1,) + shape, lambda b, i: (b,) + (0,) * len(shape))
    f32 = jnp.float32
    return pl.pallas_call(
        _ssd_kernel,
        grid=(bsz, nt),
        in_specs=[pl.BlockSpec((t, SSD_IN_W), lambda b, i: (b * nt + i, 0)),
                  per_b((keep, SSD_CONV_CH)), per_b((SSD_GROUPS, gw, SSD_STATE)),
                  _resident((CONV_W, SSD_CONV_CH)), _resident((1, SSD_CONV_CH)),
                  _resident((1, DT_PAD)), _resident((1, DT_PAD)),
                  _resident((1, SSD_WIDTH)), _resident((1, SSD_WIDTH))],
        out_specs=[pl.BlockSpec((t, SSD_WIDTH), lambda b, i: (b * nt + i, 0)),
                   per_b((keep, SSD_CONV_CH)), per_b((SSD_GROUPS, gw, SSD_STATE))],
        out_shape=[jax.ShapeDtypeStruct((n, SSD_WIDTH), jnp.bfloat16),
                   jax.ShapeDtypeStruct((bsz, keep, SSD_CONV_CH), f32),
                   jax.ShapeDtypeStruct((bsz, SSD_GROUPS, gw, SSD_STATE), f32)],
        scratch_shapes=[pltpu.VMEM((CONV_PAD + t, SSD_CONV_CH), f32),
                        pltpu.VMEM((SSD_GROUPS, gw, SSD_STATE), f32)],
        compiler_params=_params(_ARB, _ARB),
        name="ssd",
    )(ssd_in, conv0, h0, cw, cb, dtb, aneg, dskip, ng)


def _stack_q(q):
    lane = lax.broadcasted_iota(jnp.int32, q.shape, 1)
    zero = jnp.zeros_like(q)
    return jnp.concatenate([jnp.where(lane < ATT_HEAD_DIM, q, zero),
                            jnp.where(lane < ATT_HEAD_DIM, zero, q)], axis=0)


def _diff_lambda(lv, lam_init):
    e1 = jnp.exp(jnp.sum(lv[0:1, :] * lv[1:2, :], axis=1, keepdims=True))
    e2 = jnp.exp(jnp.sum(lv[2:3, :] * lv[3:4, :], axis=1, keepdims=True))
    return e1 - e2 + lam_init


def _attn_finish(acc, l, lam, g, lam_init, bq):
    o = acc[0:bq] / l[0:bq] - lam * (acc[bq:2 * bq] / l[bq:2 * bq])
    return _rms(o, g) * (1.0 - lam_init)


def _attn_prompt_kernel(q_ref, k_ref, vt_ref, lv_ref, g_ref, o_ref, acc0_ref, acc1_ref, s_ref,
                        *, lam_init, bq):
    i = pl.program_id(2)
    accs = (acc0_ref, acc1_ref)
    qs = _stack_q(q_ref[...])
    qs = (qs[0:bq], qs[bq:2 * bq])
    for a in accs:
        a[...] = jnp.zeros(a.shape, jnp.float32)

    def scores(j):
        k = k_ref[pl.ds(pl.multiple_of(j * bq, bq), bq), :]
        return tuple(lax.dot_general(k, qs[mp], (((1,), (1,)), ((), ())),
                                     preferred_element_type=jnp.float32) for mp in range(2))

    def consume(j, slot, mask, stats):
        vt = vt_ref[j]
        out = []
        for mp in range(2):
            m_old = stats[mp]
            s = s_ref[2 * slot + mp]
            if mask is not None:
                s = jnp.where(mask, s, NEG_BIG)
            m_new = jnp.maximum(m_old, jnp.max(s, axis=0, keepdims=True))
            p = jnp.exp2(s - m_new)
            alpha = jnp.exp2(m_old - m_new)
            accs[mp][...] = alpha * accs[mp][...] + jnp.dot(vt, p.astype(jnp.bfloat16),
                                                            preferred_element_type=jnp.float32)
            out.append(m_new)
        return tuple(out)

    def store_scores(s, slot):
        for mp in range(2):
            s_ref[2 * slot + mp] = s[mp]

    store_scores(scores(0), 0)

    def step(j, slot, stats):
        s_next = scores(j + 1)
        stats = consume(j, slot, None, stats)
        store_scores(s_next, 1 - slot)
        return stats

    def pair(jj, stats):
        return step(2 * jj + 1, 1, step(2 * jj, 0, stats))

    init = (jnp.full((1, bq), NEG_BIG, jnp.float32),) * 2
    stats = lax.fori_loop(0, lax.shift_right_logical(i, 1), pair, init)
    kpos = lax.broadcasted_iota(jnp.int32, (bq, bq), 0)
    qpos = lax.broadcasted_iota(jnp.int32, (bq, bq), 1)
    mask = (kpos // CHUNK) <= (qpos // CHUNK)

    def last_even(stats):
        return consume(i, 0, mask, stats)

    def last_odd(stats):
        return consume(i, 1, mask, step(i - 1, 0, stats))

    lax.cond(i % 2 == 1, last_odd, last_even, stats)
    lam = _diff_lambda(lv_ref[...], lam_init)
    d = ATT_V_DIM
    o = (acc0_ref[0:d, :] / acc0_ref[d:d + 1, :]
         - lam * (acc1_ref[0:d, :] / acc1_ref[d:d + 1, :]))
    o = o * lax.rsqrt(jnp.mean(o * o, axis=0, keepdims=True) + EPS) * g_ref[...] * (1.0 - lam_init)
    o_ref[...] = o.T.astype(o_ref.dtype)


def _attn_prompt(q, kb, vt, lv, g, lam_init, bsz, bq):
    n = q.shape[0]
    seq = n // bsz
    nq = seq // bq
    assert bq % CHUNK == 0 and seq % bq == 0 and vt.shape == (n // bq, ATT_HEADS * ATT_VT_ROWS, bq)
    f32 = jnp.float32
    return pl.pallas_call(
        functools.partial(_attn_prompt_kernel, lam_init=lam_init, bq=bq),
        grid=(bsz, ATT_HEADS, nq),
        in_specs=[pl.BlockSpec((bq, ATT_V_DIM), lambda b, h, i: (b * nq + i, h)),
                  pl.BlockSpec((seq, ATT_V_DIM), lambda b, h, i: (b, h)),
                  pl.BlockSpec((nq, ATT_VT_ROWS, bq), lambda b, h, i: (b, h, 0)),
                  pl.BlockSpec((4, ATT_HEAD_DIM), lambda b, h, i: (0, 0)),
                  pl.BlockSpec((ATT_V_DIM, 1), lambda b, h, i: (0, 0))],
        out_specs=pl.BlockSpec((bq, ATT_V_DIM), lambda b, h, i: (b * nq + i, h)),
        out_shape=jax.ShapeDtypeStruct((n, ATT_WIDTH), jnp.bfloat16),
        scratch_shapes=[pltpu.VMEM((ATT_VT_ROWS, bq), f32), pltpu.VMEM((ATT_VT_ROWS, bq), f32),
                        pltpu.VMEM((4, bq, bq), f32)],
        compiler_params=_params(_ARB, _ARB, _ARB),
        name="attn_prompt",
    )(q, kb, vt, lv, g)


def _attn_sample_kernel(q_ref, kn_ref, vn_ref, kp_ref, vp_ref, lv_ref, g_ref, o_ref, *, lam_init, lq):
    lam = _diff_lambda(lv_ref[...], lam_init)
    bf16 = jnp.bfloat16
    nt = (((1,), (1,)), ((), ()))
    past_len = kp_ref.shape[0] // ATT_HEADS
    outs = []
    for h in range(ATT_HEADS):
        sl = slice(h * ATT_V_DIM, (h + 1) * ATT_V_DIM)
        qs = _stack_q(q_ref[:, sl])
        kp = kp_ref[pl.ds(h, past_len, stride=ATT_HEADS), :].astype(bf16)
        vp = vp_ref[pl.ds(h, past_len, stride=ATT_HEADS), :].astype(bf16)
        s_p = lax.dot_general(qs, kp, nt, preferred_element_type=jnp.float32)
        s_n = lax.dot_general(qs, kn_ref[:, sl], nt, preferred_element_type=jnp.float32)
        m = jnp.maximum(jnp.max(s_p, axis=1, keepdims=True), jnp.max(s_n, axis=1, keepdims=True))
        p_p = jnp.exp2(s_p - m)
        p_n = jnp.exp2(s_n - m)
        l = jnp.sum(p_p, axis=1, keepdims=True) + jnp.sum(p_n, axis=1, keepdims=True)
        acc = (jnp.dot(p_p.astype(bf16), vp, preferred_element_type=jnp.float32)
               + jnp.dot(p_n.astype(bf16), vn_ref[:, sl], preferred_element_type=jnp.float32))
        outs.append(_attn_finish(acc, l, lam, g_ref[...], lam_init, lq))
    o_ref[...] = jnp.concatenate(outs, axis=1).astype(o_ref.dtype)


def _attn_sample(q, kb, vb, past_k, past_v, layer, lv, g, lam_init, bsz, lq, past_len):
    assert past_len % CHUNK == 0 and lq <= CHUNK
    n = q.shape[0]
    step = lambda: pl.BlockSpec((lq, ATT_WIDTH), lambda b: (b, 0))
    past = lambda: pl.BlockSpec((past_len * ATT_HEADS, ATT_V_DIM), lambda b: (layer * bsz + b, 0))
    return pl.pallas_call(
        functools.partial(_attn_sample_kernel, lam_init=lam_init, lq=lq),
        grid=(bsz,),
        in_specs=[step(), step(), step(), past(), past(),
                  pl.BlockSpec((4, ATT_HEAD_DIM), lambda b: (0, 0)),
                  pl.BlockSpec((1, ATT_V_DIM), lambda b: (0, 0))],
        out_specs=step(),
        out_shape=jax.ShapeDtypeStruct((n, ATT_WIDTH), jnp.bfloat16),
        compiler_params=_params(_ARB),
        name="attn_sample",
    )(q, kb, vb, past_k, past_v, lv, g)


def _out_ffn_tile(x, mixed, wo_ref, gf_ref, wu_ref, wd_ref, gn_ref, ff_chunk, final_norm):
    out = None
    for stage in _out_ffn_stages(x, mixed, wo_ref, gf_ref, wu_ref, wd_ref, gn_ref, ff_chunk, final_norm):
        out = stage()
    return out


def _out_ffn_stages(x, mixed, wo_ref, gf_ref, wu_ref, wd_ref, gn_ref, ff_chunk, final_norm):
    f32 = jnp.float32
    st = {}
    n_chunks = D_FF // ff_chunk

    def proj():
        st["x"] = x + jnp.dot(mixed, wo_ref[...], preferred_element_type=f32)
        st["hn"] = _rms(st["x"], gf_ref[...]).astype(jnp.bfloat16)

    def mlp(c):
        u = jnp.dot(st["hn"], wu_ref[:, c * ff_chunk:(c + 1) * ff_chunk], preferred_element_type=f32)
        u = jnp.square(jnp.maximum(u, 0.0)).astype(jnp.bfloat16)
        st["x"] = st["x"] + jnp.dot(u, wd_ref[c * ff_chunk:(c + 1) * ff_chunk, :], preferred_element_type=f32)
        if c < n_chunks - 1:
            return None
        return _rms(st["x"], gn_ref[...]) if final_norm else st["x"]

    return [proj] + [functools.partial(mlp, c) for c in range(n_chunks)]


def _out_ffn_kernel(x_ref, yl_ref, ys_ref, ya_ref, wo_ref, gf_ref, wu_ref, wd_ref, gn_ref, o_ref,
                    *, ff_chunk, final_norm):
    mixed = jnp.concatenate([yl_ref[...], ys_ref[...], ya_ref[...]], axis=1)
    o_ref[...] = _out_ffn_tile(x_ref[...], mixed, wo_ref, gf_ref, wu_ref, wd_ref, gn_ref,
                               ff_chunk, final_norm)


def _rec_ffn_kernel(lru_in_ref, ssd_in_ref, x_ref, ya_ref,
                    lconv0_ref, lh0_ref, sconv0_ref, sh0_ref,
                    lcw_ref, lcb_ref, wg_ref, bg_ref, nsp_ref,
                    scw_ref, scb_ref, dtb_ref, aneg_ref, dskip_ref, ng_ref,
                    wo_ref, gf_ref, wu_ref, wd_ref, gn_ref,
                    o_ref, lconv_out_ref, lh_out_ref, sconv_out_ref, sh_out_ref,
                    ymix_ref, lxp_ref, lh_ref, sxp_ref, sh_ref, *, ff_chunk, final_norm, ssd_chunk):
    i = pl.program_id(1)
    tiles = pl.num_programs(1) - 1
    first = i == 0
    _state_init(first, lxp_ref, lconv0_ref, lh_ref, lh0_ref)
    _state_init(first, sxp_ref, sconv0_ref, sh_ref, sh0_ref)

    @pl.when(jnp.logical_and(first, pl.program_id(0) == 0))
    def _():
        ymix_ref[...] = jnp.zeros(ymix_ref.shape, ymix_ref.dtype)

    mixed = jnp.concatenate([ymix_ref[(i + 1) % 2], ya_ref[...]], axis=1)
    ffn = _out_ffn_stages(x_ref[...], mixed, wo_ref, gf_ref, wu_ref, wd_ref, gn_ref, ff_chunk, final_norm)
    rec = ([functools.partial(_lru_tile, lru_in_ref, lcw_ref, lcb_ref, wg_ref, bg_ref, nsp_ref, lxp_ref, lh_ref)]
           + _ssd_stages(ssd_in_ref, scw_ref, scb_ref, dtb_ref, aneg_ref, dskip_ref, ng_ref,
                         sxp_ref, sh_ref, ssd_chunk))
    assert len(ffn) >= len(rec)
    after = [(r * len(ffn)) // len(rec) for r in range(len(rec))]
    x_new = y_lru = y_ssd = None
    for k in range(len(ffn)):
        x_new = ffn[k]()
        for r in range(len(rec)):
            if after[r] == k:
                y_ssd = rec[r]()
                if r == 0:
                    y_lru = y_ssd
    o_ref[...] = x_new
    ymix_ref[i % 2] = jnp.concatenate([y_lru, y_ssd], axis=1).astype(ymix_ref.dtype)
    last = i == tiles - 1
    _state_emit(last, lxp_ref, lconv_out_ref, lh_ref, lh_out_ref)
    _state_emit(last, sxp_ref, sconv_out_ref, sh_ref, sh_out_ref)


def _rec_ffn(lru_in, ssd_in, x, ya, states, p, bsz, t, ssd_chunk, final_norm, ff_chunk=1024):
    n = x.shape[0]
    nt = n // bsz // t
    keep = CONV_W - 1
    gw = 2 * SSD_HEAD_DIM
    f32 = jnp.float32
    rec_row = lambda width: pl.BlockSpec((t, width), lambda b, i: (b * nt + jnp.minimum(i, nt - 1), 0))
    ffn_row = lambda width: pl.BlockSpec((t, width), lambda b, i: (b * nt + jnp.maximum(i - 1, 0), 0))
    per_b = lambda shape: pl.BlockSpec((1,) + shape, lambda b, i: (b,) + (0,) * len(shape))
    state_specs = [per_b((keep, LRU_WIDTH)), per_b((1, LRU_WIDTH)),
                   per_b((keep, SSD_CONV_CH)), per_b((SSD_GROUPS, gw, SSD_STATE))]
    return pl.pallas_call(
        functools.partial(_rec_ffn_kernel, ff_chunk=ff_chunk, final_norm=final_norm, ssd_chunk=ssd_chunk),
        grid=(bsz, nt + 1),
        in_specs=[rec_row(LRU_IN_W), rec_row(SSD_IN_W), ffn_row(D_MODEL), ffn_row(ATT_WIDTH)] + state_specs
                 + [_resident((CONV_W, LRU_WIDTH)), _resident((1, LRU_WIDTH)),
                    _resident((LRU_WIDTH, 2 * LRU_WIDTH)), _resident((1, 2 * LRU_WIDTH)),
                    _resident((1, LRU_WIDTH)),
                    _resident((CONV_W, SSD_CONV_CH)), _resident((1, SSD_CONV_CH)),
                    _resident((1, DT_PAD)), _resident((1, DT_PAD)),
                    _resident((1, SSD_WIDTH)), _resident((1, SSD_WIDTH)),
                    _resident((D_MODEL, D_MODEL)), _resident((1, D_MODEL)),
                    _resident((D_MODEL, D_FF)), _resident((D_FF, D_MODEL)), _resident((1, D_MODEL))],
        out_specs=[ffn_row(D_MODEL)] + state_specs,
        out_shape=[jax.ShapeDtypeStruct((n, D_MODEL), f32),
                   jax.ShapeDtypeStruct((bsz, keep, LRU_WIDTH), f32),
                   jax.ShapeDtypeStruct((bsz, 1, LRU_WIDTH), f32),
                   jax.ShapeDtypeStruct((bsz, keep, SSD_CONV_CH), f32),
                   jax.ShapeDtypeStruct((bsz, SSD_GROUPS, gw, SSD_STATE), f32)],
        scratch_shapes=[pltpu.VMEM((2, t, LRU_WIDTH + SSD_WIDTH), jnp.bfloat16),
                        pltpu.VMEM((CONV_PAD + t, LRU_WIDTH), f32), pltpu.VMEM((1, LRU_WIDTH), f32),
                        pltpu.VMEM((CONV_PAD + t, SSD_CONV_CH), f32),
                        pltpu.VMEM((SSD_GROUPS, gw, SSD_STATE), f32)],
        compiler_params=_params(_ARB, _ARB),
        name="rec_ffn",
    )(lru_in, ssd_in, x, ya, *states,
      p["lru_conv_w"], p["lru_conv_b"], p["lru_wg"], p["lru_bg"], p["lru_nsp"],
      p["ssd_conv_w"], p["ssd_conv_b"], p["ssd_dt_bias"], p["ssd_a_neg"], p["ssd_d"], p["ssd_norm_g"],
      p["w_out"], p["norm_ffn_g"], p["w_up"], p["w_down"], p["norm_f_g"])


def _out_ffn(x, yl, ys, ya, wo, gf, wu, wd, gn, tm, final_norm, ff_chunk=1024):
    n = x.shape[0]
    row = lambda width: pl.BlockSpec((tm, width), lambda i: (i, 0))
    return pl.pallas_call(
        functools.partial(_out_ffn_kernel, ff_chunk=ff_chunk, final_norm=final_norm),
        grid=(n // tm,),
        in_specs=[row(D_MODEL), row(LRU_WIDTH), row(SSD_WIDTH), row(ATT_WIDTH),
                  _resident((D_MODEL, D_MODEL)), _resident((1, D_MODEL)),
                  _resident((D_MODEL, D_FF)), _resident((D_FF, D_MODEL)), _resident((1, D_MODEL))],
        out_specs=row(D_MODEL),
        out_shape=jax.ShapeDtypeStruct((n, D_MODEL), jnp.float32),
        compiler_params=_params(_ARB),
        name="out_ffn",
    )(x, yl, ys, ya, wo, gf, wu, wd, gn)


def _prep_w_in(w_in):
    o_dt = 2 * LRU_WIDTH + SSD_WIDTH + SSD_CONV_CH
    head = w_in[..., :o_dt]
    dt = jnp.pad(w_in[..., o_dt:o_dt + SSD_HEADS], ((0, 0), (0, 0), (0, DT_PAD - SSD_HEADS)))
    tail = w_in[..., o_dt + SSD_HEADS:]
    return jnp.concatenate([head, dt, tail], axis=-1).astype(jnp.bfloat16)


def _block_diag(w):
    eye = jnp.eye(LRU_BLOCKS, dtype=w.dtype)
    full = w[:, :, :, None, :] * eye[None, :, None, :, None]
    return full.reshape(w.shape[0], LRU_WIDTH, LRU_WIDTH)


def _pad_heads(v):
    return jnp.pad(v, ((0, 0), (0, DT_PAD - SSD_HEADS)))[:, None, :]


def _layer(x, states, kv_all, layer, p, lam_init, bsz, tiles, attn, vt_block, final_norm):
    tm, t_lru, t_ssd, tm_ffn = tiles
    conv_lru0, h_lru0, conv_ssd0, h_ssd0 = states
    lru_in, ssd_in, q, k_all, v_all, kb, vb = _in_proj(x, p["norm_mix_g"], p["w_in"],
                                                       kv_all[0], kv_all[1], layer, tm, vt_block)
    if tm_ffn is None:
        x, conv_lru, h_lru, conv_ssd, h_ssd = _rec_ffn(lru_in, ssd_in, x, attn(q, kb, vb), states, p,
                                                       bsz, t_lru, t_ssd, final_norm)
        return x, (k_all, v_all), (conv_lru, h_lru, conv_ssd, h_ssd)
    y_lru, conv_lru, h_lru = _lru(lru_in, conv_lru0, h_lru0, p["lru_conv_w"], p["lru_conv_b"],
                                  p["lru_wg"], p["lru_bg"], p["lru_nsp"], bsz, t_lru)
    y_ssd, conv_ssd, h_ssd = _ssd(ssd_in, conv_ssd0, h_ssd0, p["ssd_conv_w"], p["ssd_conv_b"],
                                  p["ssd_dt_bias"], p["ssd_a_neg"], p["ssd_d"], p["ssd_norm_g"], bsz, t_ssd)
    y_att = attn(q, kb, vb)
    x = _out_ffn(x, y_lru, y_ssd, y_att, p["w_out"], p["norm_ffn_g"], p["w_up"], p["w_down"],
                 p["norm_f_g"], tm_ffn, final_norm)
    return x, (k_all, v_all), (conv_lru, h_lru, conv_ssd, h_ssd)


def _forward(x_prompt, x_sample, cache_att_k, cache_att_v, state_lru_conv, state_lru_h,
             state_ssd_conv, state_ssd_h, norm_mix_g, w_in, lru_conv_w, lru_conv_b,
             lru_wa, lru_ba, lru_wx, lru_bx, lru_lambda, ssd_conv_w, ssd_conv_b,
             ssd_dt_bias, ssd_a_log, ssd_d, ssd_norm_g, att_lambda, att_subln_g,
             w_out, norm_ffn_g, w_up, w_down, norm_f_g, *, tiles_p, tiles_s, bq):
    f32, bf16 = jnp.float32, jnp.bfloat16
    depth = w_in.shape[0]
    bp, seq, _ = x_prompt.shape
    bs, lq, _ = x_sample.shape
    past_len = cache_att_k.shape[2]
    gw = 2 * SSD_HEAD_DIM

    stacked = {
        "norm_mix_g": norm_mix_g[:, None, :],
        "w_in": _prep_w_in(w_in),
        "lru_conv_w": lru_conv_w, "lru_conv_b": lru_conv_b[:, None, :],
        "lru_wg": jnp.concatenate([_block_diag(lru_wa), _block_diag(lru_wx)], axis=-1).astype(bf16),
        "lru_bg": jnp.concatenate([lru_ba, lru_bx], axis=-1)[:, None, :],
        "lru_nsp": (-LRU_C * jax.nn.softplus(-lru_lambda.astype(f32)))[:, None, :],
        "ssd_conv_w": ssd_conv_w, "ssd_conv_b": ssd_conv_b[:, None, :],
        "ssd_dt_bias": _pad_heads(ssd_dt_bias),
        "ssd_a_neg": _pad_heads(-jnp.exp(ssd_a_log.astype(f32))),
        "ssd_d": jnp.repeat(ssd_d, SSD_HEAD_DIM, axis=-1)[:, None, :],
        "ssd_norm_g": ssd_norm_g[:, None, :],
        "att_lambda": att_lambda, "att_subln_g": att_subln_g[:, None, :],
        "att_subln_g_col": att_subln_g[:, :, None],
        "w_out": w_out.astype(bf16), "norm_ffn_g": norm_ffn_g[:, None, :],
        "w_up": w_up.astype(bf16), "w_down": w_down.astype(bf16),
    }
    gn = norm_f_g[None, :]

    xp = x_prompt.reshape(bp * seq, D_MODEL)
    xs = x_sample.reshape(bs * lq, D_MODEL)
    past_k = cache_att_k.reshape(depth * bs * past_len * ATT_HEADS, ATT_V_DIM)
    past_v = cache_att_v.reshape(depth * bs * past_len * ATT_HEADS, ATT_V_DIM)
    zero_states = (jnp.zeros((bp, CONV_W - 1, LRU_WIDTH), f32), jnp.zeros((bp, 1, LRU_WIDTH), f32),
                   jnp.zeros((bp, CONV_W - 1, SSD_CONV_CH), f32),
                   jnp.zeros((bp, SSD_GROUPS, gw, SSD_STATE), f32))
    kv_p = (jnp.zeros((depth * bp * seq * ATT_HEADS, ATT_V_DIM), f32),) * 2
    kv_s = (jnp.zeros((depth * bs * lq * ATT_HEADS, ATT_V_DIM), f32),) * 2
    st_p, st_s = [], []
    for l in range(depth):
        p = {name: v[l] for name, v in stacked.items()}
        p["norm_f_g"] = gn
        lam_init = 0.8 - 0.6 * math.exp(-0.3 * l)
        last = l == depth - 1
        attn_p = functools.partial(_attn_prompt, lv=p["att_lambda"], g=p["att_subln_g_col"],
                                   lam_init=lam_init, bsz=bp, bq=bq)
        xp, kv_p, sp = _layer(xp, zero_states, kv_p, l, p, lam_init, bp, tiles_p, attn_p, bq, last)
        states_s = (state_lru_conv[l], state_lru_h[l][:, None, :], state_ssd_conv[l],
                    state_ssd_h[l].reshape(bs, SSD_GROUPS, gw, SSD_STATE))
        attn_s = functools.partial(_attn_sample, past_k=past_k, past_v=past_v, layer=l,
                                   lv=p["att_lambda"], g=p["att_subln_g"], lam_init=lam_init,
                                   bsz=bs, lq=lq, past_len=past_len)
        xs, kv_s, ss = _layer(xs, states_s, kv_s, l, p, lam_init, bs, tiles_s, attn_s, None, last)
        st_p.append(sp)
        st_s.append(ss)

    def collect(kv, sts, bsz, length):
        stack = lambda i: jnp.stack([s[i] for s in sts], axis=0)
        return (kv[0].reshape(depth, bsz, length, ATT_HEADS, ATT_V_DIM),
                kv[1].reshape(depth, bsz, length, ATT_HEADS, ATT_V_DIM),
                stack(0), stack(1).reshape(depth, bsz, LRU_WIDTH), stack(2),
                stack(3).reshape(depth, bsz, SSD_HEADS, SSD_HEAD_DIM, SSD_STATE))

    return ((xp.reshape(bp, seq, D_MODEL), xs.reshape(bs, lq, D_MODEL))
            + collect(kv_p, st_p, bp, seq) + collect(kv_s, st_s, bs, lq))


def kernel(x_prompt, x_sample, cache_att_k, cache_att_v, state_lru_conv, state_lru_h, state_ssd_conv, state_ssd_h, norm_mix_g, w_in, lru_conv_w, lru_conv_b, lru_wa, lru_ba, lru_wx, lru_bx, lru_lambda, ssd_conv_w, ssd_conv_b, ssd_dt_bias, ssd_a_log, ssd_d, ssd_norm_g, att_lambda, att_subln_g, w_out, norm_ffn_g, w_up, w_down, norm_f_g):
    lq = x_sample.shape[1]
    n_s = x_sample.shape[0] * lq
    return _forward(x_prompt, x_sample, cache_att_k, cache_att_v, state_lru_conv, state_lru_h,
                    state_ssd_conv, state_ssd_h, norm_mix_g, w_in, lru_conv_w, lru_conv_b,
                    lru_wa, lru_ba, lru_wx, lru_bx, lru_lambda, ssd_conv_w, ssd_conv_b,
                    ssd_dt_bias, ssd_a_log, ssd_d, ssd_norm_g, att_lambda, att_subln_g,
                    w_out, norm_ffn_g, w_up, w_down, norm_f_g,
                    tiles_p=(512, 512, 256, None), tiles_s=(n_s, lq, lq, n_s), bq=1024)
```

```python
import functools
import math

import jax
import jax.numpy as jnp
from jax import lax
from jax.experimental import pallas as pl
from jax.experimental.pallas import tpu as pltpu

D_MODEL = 1024
DEPTH = 4
CHUNK = 64
CONV_W = 4
LRU_WIDTH = 256
LRU_BLOCKS = 4
LRU_BLOCK = LRU_WIDTH // LRU_BLOCKS
LRU_C = 8.0
SSD_WIDTH = 256
SSD_HEAD_DIM = 64
SSD_HEADS = 4
SSD_GROUPS = 2
SSD_STATE = 128
SSD_CONV_CH = SSD_WIDTH + 2 * SSD_GROUPS * SSD_STATE
ATT_WIDTH = 512
ATT_HEADS = 4
ATT_V_DIM = 128
ATT_HEAD_DIM = 64
ATT_SCALE = ATT_HEAD_DIM ** -0.5
LOG2_E = math.log2(math.e)
D_FF = 4 * D_MODEL
EPS = 1e-6

LANES = 128
BF16_SUBLANES = 16
ATT_VT_ROWS = ATT_V_DIM + BF16_SUBLANES
DIAG_STRIP = 256
CONV_PAD = 8
DT_PAD = LANES
LRU_IN_W = 2 * LRU_WIDTH
SSD_IN_W = SSD_WIDTH + SSD_CONV_CH + DT_PAD
IN_COLS_PADDED = LRU_IN_W + SSD_IN_W + 3 * ATT_WIDTH
NEG_BIG = -1e30
VMEM_LIMIT = 56 * 1024 * 1024

_ARB = "arbitrary"


def _params(*sem):
    return pltpu.CompilerParams(dimension_semantics=sem, vmem_limit_bytes=VMEM_LIMIT)


def _resident(shape):
    nd = len(shape)
    return pl.BlockSpec(shape, lambda *_: (0,) * nd, pipeline_mode=pl.Buffered(1))


def _rms(x, g):
    return x * lax.rsqrt(jnp.mean(x * x, axis=-1, keepdims=True) + EPS) * g


def _in_proj_kernel(x_ref, g_ref, w_ref, k_all_ref, v_all_ref,
                    lru_ref, ssd_ref, q_ref, k_ref, v_ref, kb_ref, vb_ref):
    del k_all_ref, v_all_ref
    tm = x_ref.shape[0]
    xn = _rms(x_ref[...], g_ref[...]).astype(jnp.bfloat16)

    def store_heads(ref, val):
        for h in range(ATT_HEADS):
            ref[pl.ds(h, tm, stride=ATT_HEADS), :] = val[:, h * ATT_V_DIM:(h + 1) * ATT_V_DIM]


    def seg(lo, width):
        return jnp.dot(xn, w_ref[:, lo:lo + width], preferred_element_type=jnp.float32)

    lru_ref[...] = seg(0, LRU_IN_W)
    ssd_ref[...] = seg(LRU_IN_W, SSD_IN_W)
    lo = LRU_IN_W + SSD_IN_W
    q_ref[...] = (seg(lo, ATT_WIDTH) * (ATT_SCALE * LOG2_E)).astype(jnp.bfloat16)
    k = seg(lo + ATT_WIDTH, ATT_WIDTH)
    store_heads(k_ref, k)
    kb_ref[...] = k.astype(jnp.bfloat16)
    v = seg(lo + 2 * ATT_WIDTH, ATT_WIDTH)
    store_heads(v_ref, v)
    if len(vb_ref.shape) == 2:
        vb_ref[...] = v.astype(jnp.bfloat16)
    else:
        ones_rows = (lax.broadcasted_iota(jnp.int32, (BF16_SUBLANES, tm), 0) == 0).astype(jnp.bfloat16)
        for h in range(ATT_HEADS):
            base = h * ATT_VT_ROWS
            vb_ref[0, base:base + ATT_V_DIM, :] = v[:, h * ATT_V_DIM:(h + 1) * ATT_V_DIM].T.astype(jnp.bfloat16)
            vb_ref[0, base + ATT_V_DIM:base + ATT_VT_ROWS, :] = ones_rows


def _in_proj(x, g, w, k_all, v_all, layer, tm, bk):
    n = x.shape[0]
    nt = n // tm
    row = lambda width: pl.BlockSpec((tm, width), lambda i: (i, 0))
    heads_rows = pl.BlockSpec((ATT_HEADS * tm, ATT_V_DIM), lambda i: (layer * nt + i, 0))
    untouched = pl.BlockSpec(memory_space=pl.ANY)
    all_shape = jax.ShapeDtypeStruct(k_all.shape, k_all.dtype)
    f32, bf16 = jnp.float32, jnp.bfloat16
    if bk is None:
        vb_spec, vb_shape = row(ATT_WIDTH), jax.ShapeDtypeStruct((n, ATT_WIDTH), bf16)
    else:
        per_block = bk // tm
        assert per_block * tm == bk
        vb_spec = pl.BlockSpec((1, ATT_HEADS * ATT_VT_ROWS, tm), lambda i: (i // per_block, 0, i % per_block))
        vb_shape = jax.ShapeDtypeStruct((n // bk, ATT_HEADS * ATT_VT_ROWS, bk), bf16)
    return pl.pallas_call(
        _in_proj_kernel,
        grid=(nt,),
        in_specs=[row(D_MODEL), _resident((1, D_MODEL)), _resident((D_MODEL, IN_COLS_PADDED)),
                  untouched, untouched],
        out_specs=[row(LRU_IN_W), row(SSD_IN_W), row(ATT_WIDTH), heads_rows, heads_rows,
                   row(ATT_WIDTH), vb_spec],
        out_shape=[jax.ShapeDtypeStruct((n, LRU_IN_W), f32), jax.ShapeDtypeStruct((n, SSD_IN_W), f32),
                   jax.ShapeDtypeStruct((n, ATT_WIDTH), bf16), all_shape, all_shape,
                   jax.ShapeDtypeStruct((n, ATT_WIDTH), bf16), vb_shape],
        input_output_aliases={3: 3, 4: 4},
        compiler_params=_params(_ARB),
        name="in_proj",
    )(x, g, w, k_all, v_all)


def _state_init(first, xp_ref, conv0_ref, h_ref, h0_ref):
    t = xp_ref.shape[0] - CONV_PAD
    keep = CONV_W - 1

    @pl.when(first)
    def _():
        xp_ref[CONV_PAD - keep:CONV_PAD, :] = conv0_ref[0]
        h_ref[...] = h0_ref[0]

    @pl.when(jnp.logical_not(first))
    def _():
        xp_ref[CONV_PAD - keep:CONV_PAD, :] = xp_ref[CONV_PAD + t - keep:CONV_PAD + t, :]


def _state_emit(last, xp_ref, conv_out_ref, h_ref, h_out_ref):
    t = xp_ref.shape[0] - CONV_PAD
    keep = CONV_W - 1

    @pl.when(last)
    def _():
        conv_out_ref[0] = xp_ref[CONV_PAD + t - keep:CONV_PAD + t, :]
        h_out_ref[0] = h_ref[...]


def _causal_conv(xp_ref, x, w_ref, b_ref):
    t = x.shape[0]
    keep = CONV_W - 1
    xp_ref[CONV_PAD:CONV_PAD + t, :] = x
    y = b_ref[...] + w_ref[CONV_W - 1:CONV_W, :] * x
    for j in range(CONV_W - 1):
        y = y + w_ref[j:j + 1, :] * xp_ref[CONV_PAD - keep + j:CONV_PAD - keep + j + t, :]
    return y


def _row_scan(step, vals, t):
    row = lax.broadcasted_iota(jnp.int32, (t, 1), 0)
    d = 1
    while d < t:
        shifted = tuple(pltpu.roll(v, d, 0) for v in vals)
        new = step(shifted, vals)
        vals = tuple(jnp.where(row >= d, n, v) for n, v in zip(new, vals))
        d *= 2
    return vals


def _lru_tile(in_ref, cw_ref, cb_ref, wg_ref, bg_ref, nsp_ref, xp_ref, h_ref):
    t = in_ref.shape[0]
    gate = in_ref[:, 0:LRU_WIDTH]
    x = _causal_conv(xp_ref, in_ref[:, LRU_WIDTH:LRU_IN_W], cw_ref, cb_ref)
    pre = jnp.dot(x.astype(jnp.bfloat16), wg_ref[...], preferred_element_type=jnp.float32) + bg_ref[...]
    r = jax.nn.sigmoid(pre[:, 0:LRU_WIDTH])
    ig = jax.nn.sigmoid(pre[:, LRU_WIDTH:2 * LRU_WIDTH])
    log_a = r * nsp_ref[...]
    a = jnp.exp(log_a)
    b = jnp.sqrt(-jnp.tanh(log_a) * (a * a + 1.0)) * (ig * x)

    def step(shifted, cur):
        a_s, b_s = shifted
        a_c, b_c = cur
        return a_c * a_s, a_c * b_s + b_c

    a_cum, b_cum = _row_scan(step, (a, b), t)
    h = a_cum * h_ref[...] + b_cum
    h_ref[...] = h[t - 1:t, :]
    g = gate
    gelu = 0.5 * g * (1.0 + jnp.tanh(math.sqrt(2.0 / math.pi) * (g + 0.044715 * (g * g * g))))
    return gelu * h


def _lru_kernel(in_ref, conv0_ref, h0_ref, cw_ref, cb_ref, wg_ref, bg_ref, nsp_ref,
                y_ref, conv_out_ref, h_out_ref, xp_ref, h_ref):
    i = pl.program_id(1)
    _state_init(i == 0, xp_ref, conv0_ref, h_ref, h0_ref)
    y = _lru_tile(in_ref, cw_ref, cb_ref, wg_ref, bg_ref, nsp_ref, xp_ref, h_ref)
    y_ref[...] = y.astype(y_ref.dtype)
    _state_emit(i == pl.num_programs(1) - 1, xp_ref, conv_out_ref, h_ref, h_out_ref)


def _lru(lru_in, conv0, h0, cw, cb, wg, bg, nsp, bsz, t):
    n = lru_in.shape[0]
    nt = n // bsz // t
    keep = CONV_W - 1
    per_b = lambda shape: pl.BlockSpec((1,) + shape, lambda b, i: (b,) + (0,) * len(shape))
    f32 = jnp.float32
    return pl.pallas_call(
        _lru_kernel,
        grid=(bsz, nt),
        in_specs=[pl.BlockSpec((t, LRU_IN_W), lambda b, i: (b * nt + i, 0)),
                  per_b((keep, LRU_WIDTH)), per_b((1, LRU_WIDTH)),
                  _resident((CONV_W, LRU_WIDTH)), _resident((1, LRU_WIDTH)),
                  _resident((LRU_WIDTH, 2 * LRU_WIDTH)), _resident((1, 2 * LRU_WIDTH)),
                  _resident((1, LRU_WIDTH))],
        out_specs=[pl.BlockSpec((t, LRU_WIDTH), lambda b, i: (b * nt + i, 0)),
                   per_b((keep, LRU_WIDTH)), per_b((1, LRU_WIDTH))],
        out_shape=[jax.ShapeDtypeStruct((n, LRU_WIDTH), jnp.bfloat16),
                   jax.ShapeDtypeStruct((bsz, keep, LRU_WIDTH), f32),
                   jax.ShapeDtypeStruct((bsz, 1, LRU_WIDTH), f32)],
        scratch_shapes=[pltpu.VMEM((CONV_PAD + t, LRU_WIDTH), f32), pltpu.VMEM((1, LRU_WIDTH), f32)],
        compiler_params=_params(_ARB, _ARB),
        name="lru",
    )(lru_in, conv0, h0, cw, cb, wg, bg, nsp)


def _ssd_tile(in_ref, cw_ref, cb_ref, dtb_ref, aneg_ref, dskip_ref, ng_ref, xp_ref, h_ref, chunk):
    out = None
    for stage in _ssd_stages(in_ref, cw_ref, cb_ref, dtb_ref, aneg_ref, dskip_ref, ng_ref, xp_ref, h_ref, chunk):
        out = stage()
    return out


def _ssd_stages(in_ref, cw_ref, cb_ref, dtb_ref, aneg_ref, dskip_ref, ng_ref, xp_ref, h_ref, chunk):
    t = in_ref.shape[0]
    st = {}

    def prep():
        xc = _causal_conv(xp_ref, in_ref[:, SSD_WIDTH:SSD_WIDTH + SSD_CONV_CH], cw_ref, cb_ref)
        st["xc"] = xc * jax.nn.sigmoid(xc)
        st["dt"] = jax.nn.softplus(in_ref[:, SSD_WIDTH + SSD_CONV_CH:SSD_IN_W] + dtb_ref[...])
        st["h"] = [h_ref[g] for g in range(SSD_GROUPS)]
        st["ys"] = []

    def scan(c):
        rows = slice(c * chunk, (c + 1) * chunk)
        y, st["h"] = _ssd_chunk(in_ref[rows, 0:SSD_WIDTH], st["xc"][rows], st["dt"][rows], st["h"],
                                aneg_ref, dskip_ref, ng_ref)
        st["ys"].append(y)
        if c < t // chunk - 1:
            return None
        for g in range(SSD_GROUPS):
            h_ref[g] = st["h"][g]
        ys = st["ys"]
        return ys[0] if len(ys) == 1 else jnp.concatenate(ys, axis=0)

    return [prep] + [functools.partial(scan, c) for c in range(t // chunk)]


def _ssd_chunk(z, xc, dt, h, aneg_ref, dskip_ref, ng_ref):
    t = z.shape[0]
    bf16 = jnp.bfloat16
    gw = 2 * SSD_HEAD_DIM
    (cs,) = _row_scan(lambda s, c: (s[0] + c[0],), (dt * aneg_ref[...],), t)
    cs_t = cs.T
    cs_last = cs[t - 1:t, :]
    tail = jnp.exp(cs_last - cs)
    grow = jnp.exp(cs)
    chunk_decay = jnp.exp(cs_last)

    lane = lax.broadcasted_iota(jnp.int32, (t, gw), 1)
    low = lane < SSD_HEAD_DIM
    srow = lax.broadcasted_iota(jnp.int32, (gw, 1), 0) < SSD_HEAD_DIM
    causal = lax.broadcasted_iota(jnp.int32, (t, t), 0) >= lax.broadcasted_iota(jnp.int32, (t, t), 1)

    def pick(arr, g):
        c0 = jnp.broadcast_to(arr[:, 2 * g:2 * g + 1], (t, gw))
        c1 = jnp.broadcast_to(arr[:, 2 * g + 1:2 * g + 2], (t, gw))
        return jnp.where(low, c0, c1)

    ys, h_new = [], []
    for g in range(SSD_GROUPS):
        xs = xc[:, g * gw:(g + 1) * gw]
        bm = xc[:, SSD_WIDTH + g * SSD_STATE:SSD_WIDTH + (g + 1) * SSD_STATE].astype(bf16)
        cm = xc[:, SSD_WIDTH + (SSD_GROUPS + g) * SSD_STATE:SSD_WIDTH + (SSD_GROUPS + g + 1) * SSD_STATE].astype(bf16)
        xdt = xs * pick(dt, g)
        xdt_b = xdt.astype(bf16)
        cb = lax.dot_general(cm, bm, (((1,), (1,)), ((), ())), preferred_element_type=jnp.float32)
        y_heads = []
        for hh in range(2):
            hd = 2 * g + hh
            seg = jnp.broadcast_to(cs[:, hd:hd + 1], (t, t)) - cs_t[hd:hd + 1, :]
            scores = (cb * jnp.exp(jnp.where(causal, seg, NEG_BIG))).astype(bf16)
            y_heads.append(jnp.dot(scores, xdt_b, preferred_element_type=jnp.float32))
        y_intra = jnp.where(low, y_heads[0], y_heads[1])
        h_g = h[g]
        y_inter = lax.dot_general(cm, h_g.astype(bf16), (((1,), (1,)), ((), ())),
                                  preferred_element_type=jnp.float32) * pick(grow, g)
        s_local = lax.dot_general((xdt * pick(tail, g)).astype(bf16), bm, (((0,), (0,)), ((), ())),
                                  preferred_element_type=jnp.float32)
        dec = jnp.where(srow, chunk_decay[:, 2 * g:2 * g + 1], chunk_decay[:, 2 * g + 1:2 * g + 2])
        h_new.append(dec * h_g + s_local)
        y = y_intra + y_inter + dskip_ref[:, g * gw:(g + 1) * gw] * xs
        zg = z[:, g * gw:(g + 1) * gw]
        y = y * (zg * jax.nn.sigmoid(zg))
        ys.append(_rms(y, ng_ref[:, g * gw:(g + 1) * gw]))
    return jnp.concatenate(ys, axis=1), h_new


def _ssd_kernel(in_ref, conv0_ref, h0_ref, cw_ref, cb_ref, dtb_ref, aneg_ref, dskip_ref, ng_ref,
                y_ref, conv_out_ref, h_out_ref, xp_ref, h_ref):
    i = pl.program_id(1)
    _state_init(i == 0, xp_ref, conv0_ref, h_ref, h0_ref)
    y = _ssd_tile(in_ref, cw_ref, cb_ref, dtb_ref, aneg_ref, dskip_ref, ng_ref, xp_ref, h_ref,
                  in_ref.shape[0])
    y_ref[...] = y.astype(y_ref.dtype)
    _state_emit(i == pl.num_programs(1) - 1, xp_ref, conv_out_ref, h_ref, h_out_ref)


def _ssd(ssd_in, conv0, h0, cw, cb, dtb, aneg, dskip, ng, bsz, t):
    n = ssd_in.shape[0]
    nt = n // bsz // t
    keep = CONV_W - 1
    gw = 2 * SSD_HEAD_DIM
    per_b = lambda shape: pl.BlockSpec((1,) + shape, lambda b, i: (b,) + (0,) * len(shape))
    f32 = jnp.float32
    return pl.pallas_call(
        _ssd_kernel,
        grid=(bsz, nt),
        in_specs=[pl.BlockSpec((t, SSD_IN_W), lambda b, i: (b * nt + i, 0)),
                  per_b((keep, SSD_CONV_CH)), per_b((SSD_GROUPS, gw, SSD_STATE)),
                  _resident((CONV_W, SSD_CONV_CH)), _resident((1, SSD_CONV_CH)),
                  _resident((1, DT_PAD)), _resident((1, DT_PAD)),
                  _resident((1, SSD_WIDTH)), _resident((1, SSD_WIDTH))],
        out_specs=[pl.BlockSpec((t, SSD_WIDTH), lambda b, i: (b * nt + i, 0)),
                   per_b((keep, SSD_CONV_CH)), per_b((SSD_GROUPS, gw, SSD_STATE))],
        out_shape=[jax.ShapeDtypeStruct((n, SSD_WIDTH), jnp.bfloat16),
                   jax.ShapeDtypeStruct((bsz, keep, SSD_CONV_CH), f32),
                   jax.ShapeDtypeStruct((bsz, SSD_GROUPS, gw, SSD_STATE), f32)],
        scratch_shapes=[pltpu.VMEM((CONV_PAD + t, SSD_CONV_CH), f32),
                        pltpu.VMEM((SSD_GROUPS, gw, SSD_STATE), f32)],
        compiler_params=_params(_ARB, _ARB),
        name="ssd",
    )(ssd_in, conv0, h0, cw, cb, dtb, aneg, dskip, ng)


def _stack_q(q):
    lane = lax.broadcasted_iota(jnp.int32, q.shape, 1)
    zero = jnp.zeros_like(q)
    return jnp.concatenate([jnp.where(lane < ATT_HEAD_DIM, q, zero),
                            jnp.where(lane < ATT_HEAD_DIM, zero, q)], axis=0)


def _diff_lambda(lv, lam_init):
    e1 = jnp.exp(jnp.sum(lv[0:1, :] * lv[1:2, :], axis=1, keepdims=True))
    e2 = jnp.exp(jnp.sum(lv[2:3, :] * lv[3:4, :], axis=1, keepdims=True))
    return e1 - e2 + lam_init


def _attn_finish(acc, l, lam, g, lam_init, bq):
    o = acc[0:bq] / l[0:bq] - lam * (acc[bq:2 * bq] / l[bq:2 * bq])
    return _rms(o, g) * (1.0 - lam_init)


def _attn_prompt_kernel(q_ref, k_ref, vt_ref, lv_ref, g_ref, o_ref, acc0_ref, acc1_ref, s_ref,
                        *, lam_init, bq):
    i = pl.program_id(2)
    accs = (acc0_ref, acc1_ref)
    qs = _stack_q(q_ref[...])
    qs = (qs[0:bq], qs[bq:2 * bq])
    for a in accs:
        a[...] = jnp.zeros(a.shape, jnp.float32)

    def scores(j):
        k = k_ref[pl.ds(pl.multiple_of(j * bq, bq), bq), :]
        return tuple(lax.dot_general(k, qs[mp], (((1,), (1,)), ((), ())),
                                     preferred_element_type=jnp.float32) for mp in range(2))

    def consume(j, slot, stats):
        vt = vt_ref[j]
        out = []
        for mp in range(2):
            m_old = stats[mp]
            s = s_ref[2 * slot + mp]
            m_new = jnp.maximum(m_old, jnp.max(s, axis=0, keepdims=True))
            p = jnp.exp2(s - m_new)
            alpha = jnp.exp2(m_old - m_new)
            accs[mp][...] = alpha * accs[mp][...] + jnp.dot(vt, p.astype(jnp.bfloat16),
                                                            preferred_element_type=jnp.float32)
            out.append(m_new)
        return tuple(out)

    def store_scores(s, slot):
        for mp in range(2):
            s_ref[2 * slot + mp] = s[mp]

    store_scores(scores(0), 0)

    def step(j, slot, stats):
        s_next = scores(j + 1)
        stats = consume(j, slot, stats)
        store_scores(s_next, 1 - slot)
        return stats

    def pair(jj, stats):
        return step(2 * jj + 1, 1, step(2 * jj, 0, stats))

    init = (jnp.full((1, bq), NEG_BIG, jnp.float32),) * 2
    stats = lax.fori_loop(0, lax.shift_right_logical(i, 1), pair, init)

    def consume_diag(slot, stats):
        strip = min(bq, DIAG_STRIP)
        n_strips = bq // strip
        vt = vt_ref[i]
        kpos = lax.broadcasted_iota(jnp.int32, (strip, strip), 0)
        qpos = lax.broadcasted_iota(jnp.int32, (strip, strip), 1)
        square_mask = (kpos // CHUNK) <= (qpos // CHUNK)
        for mp in range(2):
            m_old = stats[mp]
            subs, m_blk = [], None
            for r in range(n_strips):
                sub = s_ref[2 * slot + mp, r * strip:(r + 1) * strip, r * strip:]
                square = jnp.where(square_mask, sub[:, :strip], NEG_BIG)
                sub = square if r == n_strips - 1 else jnp.concatenate([square, sub[:, strip:]], axis=1)
                subs.append(sub)
                mx = jnp.max(sub, axis=0, keepdims=True)
                if r > 0:
                    mx = jnp.concatenate([jnp.full((1, r * strip), NEG_BIG, jnp.float32), mx], axis=1)
                m_blk = mx if m_blk is None else jnp.maximum(m_blk, mx)
            m_new = jnp.maximum(m_old, m_blk)
            alpha = jnp.exp2(m_old - m_new)
            total = None
            for r in range(n_strips):
                p = jnp.exp2(subs[r] - m_new[:, r * strip:]).astype(jnp.bfloat16)
                part = jnp.dot(vt[:, r * strip:(r + 1) * strip], p, preferred_element_type=jnp.float32)
                if r > 0:
                    part = jnp.concatenate([jnp.zeros((ATT_VT_ROWS, r * strip), jnp.float32), part], axis=1)
                total = part if total is None else total + part
            accs[mp][...] = alpha * accs[mp][...] + total

    @pl.when(i % 2 == 0)
    def _():
        consume_diag(0, stats)

    @pl.when(i % 2 == 1)
    def _():
        consume_diag(1, step(i - 1, 0, stats))

    lam = _diff_lambda(lv_ref[...], lam_init)
    d = ATT_V_DIM
    o = (acc0_ref[0:d, :] / acc0_ref[d:d + 1, :]
         - lam * (acc1_ref[0:d, :] / acc1_ref[d:d + 1, :]))
    o = o * lax.rsqrt(jnp.mean(o * o, axis=0, keepdims=True) + EPS) * g_ref[...] * (1.0 - lam_init)
    o_ref[...] = o.T.astype(o_ref.dtype)


def _attn_prompt(q, kb, vt, lv, g, lam_init, bsz, bq):
    n = q.shape[0]
    seq = n // bsz
    nq = seq // bq
    assert bq % CHUNK == 0 and seq % bq == 0 and vt.shape == (n // bq, ATT_HEADS * ATT_VT_ROWS, bq)
    f32 = jnp.float32
    return pl.pallas_call(
        functools.partial(_attn_prompt_kernel, lam_init=lam_init, bq=bq),
        grid=(bsz, ATT_HEADS, nq),
        in_specs=[pl.BlockSpec((bq, ATT_V_DIM), lambda b, h, i: (b * nq + i, h)),
                  pl.BlockSpec((seq, ATT_V_DIM), lambda b, h, i: (b, h)),
                  pl.BlockSpec((nq, ATT_VT_ROWS, bq), lambda b, h, i: (b, h, 0)),
                  pl.BlockSpec((4, ATT_HEAD_DIM), lambda b, h, i: (0, 0)),
                  pl.BlockSpec((ATT_V_DIM, 1), lambda b, h, i: (0, 0))],
        out_specs=pl.BlockSpec((bq, ATT_V_DIM), lambda b, h, i: (b * nq + i, h)),
        out_shape=jax.ShapeDtypeStruct((n, ATT_WIDTH), jnp.bfloat16),
        scratch_shapes=[pltpu.VMEM((ATT_VT_ROWS, bq), f32), pltpu.VMEM((ATT_VT_ROWS, bq), f32),
                        pltpu.VMEM((4, bq, bq), f32)],
        compiler_params=_params(_ARB, _ARB, _ARB),
        name="attn_prompt",
    )(q, kb, vt, lv, g)


def _attn_sample_kernel(q_ref, kn_ref, vn_ref, kp_ref, vp_ref, lv_ref, g_ref, o_ref, *, lam_init, lq):
    lam = _diff_lambda(lv_ref[...], lam_init)
    bf16 = jnp.bfloat16
    nt = (((1,), (1,)), ((), ()))
    past_len = kp_ref.shape[0] // ATT_HEADS
    outs = []
    for h in range(ATT_HEADS):
        sl = slice(h * ATT_V_DIM, (h + 1) * ATT_V_DIM)
        qs = _stack_q(q_ref[:, sl])
        kp = kp_ref[pl.ds(h, past_len, stride=ATT_HEADS), :].astype(bf16)
        vp = vp_ref[pl.ds(h, past_len, stride=ATT_HEADS), :].astype(bf16)
        s_p = lax.dot_general(qs, kp, nt, preferred_element_type=jnp.float32)
        s_n = lax.dot_general(qs, kn_ref[:, sl], nt, preferred_element_type=jnp.float32)
        m = jnp.maximum(jnp.max(s_p, axis=1, keepdims=True), jnp.max(s_n, axis=1, keepdims=True))
        p_p = jnp.exp2(s_p - m)
        p_n = jnp.exp2(s_n - m)
        l = jnp.sum(p_p, axis=1, keepdims=True) + jnp.sum(p_n, axis=1, keepdims=True)
        acc = (jnp.dot(p_p.astype(bf16), vp, preferred_element_type=jnp.float32)
               + jnp.dot(p_n.astype(bf16), vn_ref[:, sl], preferred_element_type=jnp.float32))
        outs.append(_attn_finish(acc, l, lam, g_ref[...], lam_init, lq))
    o_ref[...] = jnp.concatenate(outs, axis=1).astype(o_ref.dtype)


def _attn_sample(q, kb, vb, past_k, past_v, layer, lv, g, lam_init, bsz, lq, past_len):
    assert past_len % CHUNK == 0 and lq <= CHUNK
    n = q.shape[0]
    step = lambda: pl.BlockSpec((lq, ATT_WIDTH), lambda b: (b, 0))
    past = lambda: pl.BlockSpec((past_len * ATT_HEADS, ATT_V_DIM), lambda b: (layer * bsz + b, 0))
    return pl.pallas_call(
        functools.partial(_attn_sample_kernel, lam_init=lam_init, lq=lq),
        grid=(bsz,),
        in_specs=[step(), step(), step(), past(), past(),
                  pl.BlockSpec((4, ATT_HEAD_DIM), lambda b: (0, 0)),
                  pl.BlockSpec((1, ATT_V_DIM), lambda b: (0, 0))],
        out_specs=step(),
        out_shape=jax.ShapeDtypeStruct((n, ATT_WIDTH), jnp.bfloat16),
        compiler_params=_params(_ARB),
        name="attn_sample",
    )(q, kb, vb, past_k, past_v, lv, g)


def _out_ffn_tile(x, mixed, wo_ref, gf_ref, wu_ref, wd_ref, gn_ref, ff_chunk, final_norm):
    out = None
    for stage in _out_ffn_stages(x, mixed, wo_ref, gf_ref, wu_ref, wd_ref, gn_ref, ff_chunk, final_norm):
        out = stage()
    return out


def _out_ffn_stages(x, mixed, wo_ref, gf_ref, wu_ref, wd_ref, gn_ref, ff_chunk, final_norm):
    f32 = jnp.float32
    st = {}
    n_chunks = D_FF // ff_chunk

    def proj():
        st["x"] = x + jnp.dot(mixed, wo_ref[...], preferred_element_type=f32)
        st["hn"] = _rms(st["x"], gf_ref[...]).astype(jnp.bfloat16)

    def mlp(c):
        u = jnp.dot(st["hn"], wu_ref[:, c * ff_chunk:(c + 1) * ff_chunk], preferred_element_type=f32)
        u = jnp.square(jnp.maximum(u, 0.0)).astype(jnp.bfloat16)
        st["x"] = st["x"] + jnp.dot(u, wd_ref[c * ff_chunk:(c + 1) * ff_chunk, :], preferred_element_type=f32)
        if c < n_chunks - 1:
            return None
        return _rms(st["x"], gn_ref[...]) if final_norm else st["x"]

    return [proj] + [functools.partial(mlp, c) for c in range(n_chunks)]


def _out_ffn_kernel(x_ref, yl_ref, ys_ref, ya_ref, wo_ref, gf_ref, wu_ref, wd_ref, gn_ref, o_ref,
                    *, ff_chunk, final_norm):
    mixed = jnp.concatenate([yl_ref[...], ys_ref[...], ya_ref[...]], axis=1)
    o_ref[...] = _out_ffn_tile(x_ref[...], mixed, wo_ref, gf_ref, wu_ref, wd_ref, gn_ref,
                               ff_chunk, final_norm)


def _rec_ffn_kernel(lru_in_ref, ssd_in_ref, x_ref, ya_ref,
                    lconv0_ref, lh0_ref, sconv0_ref, sh0_ref,
                    lcw_ref, lcb_ref, wg_ref, bg_ref, nsp_ref,
                    scw_ref, scb_ref, dtb_ref, aneg_ref, dskip_ref, ng_ref,
                    wo_ref, gf_ref, wu_ref, wd_ref, gn_ref,
                    o_ref, lconv_out_ref, lh_out_ref, sconv_out_ref, sh_out_ref,
                    ymix_ref, lxp_ref, lh_ref, sxp_ref, sh_ref, *, ff_chunk, final_norm, ssd_chunk):
    i = pl.program_id(1)
    tiles = pl.num_programs(1) - 1
    first = i == 0
    _state_init(first, lxp_ref, lconv0_ref, lh_ref, lh0_ref)
    _state_init(first, sxp_ref, sconv0_ref, sh_ref, sh0_ref)

    @pl.when(jnp.logical_and(first, pl.program_id(0) == 0))
    def _():
        ymix_ref[...] = jnp.zeros(ymix_ref.shape, ymix_ref.dtype)

    mixed = jnp.concatenate([ymix_ref[(i + 1) % 2], ya_ref[...]], axis=1)
    ffn = _out_ffn_stages(x_ref[...], mixed, wo_ref, gf_ref, wu_ref, wd_ref, gn_ref, ff_chunk, final_norm)
    rec = ([functools.partial(_lru_tile, lru_in_ref, lcw_ref, lcb_ref, wg_ref, bg_ref, nsp_ref, lxp_ref, lh_ref)]
           + _ssd_stages(ssd_in_ref, scw_ref, scb_ref, dtb_ref, aneg_ref, dskip_ref, ng_ref,
                         sxp_ref, sh_ref, ssd_chunk))
    after =[(r * len(ffn)) // len(rec) for r in range(len(rec))]
    x_new = y_lru = y_ssd = None
    for k in range(len(ffn)):
        x_new = ffn[k]()
        for r in range(len(rec)):
            if after[r] == k:
                y_ssd = rec[r]()
                if r == 0:
                    y_lru = y_ssd
    o_ref[...] = x_new
    ymix_ref[i % 2] = jnp.concatenate([y_lru, y_ssd], axis=1).astype(ymix_ref.dtype)
    last = i == tiles - 1
    _state_emit(last, lxp_ref, lconv_out_ref, lh_ref, lh_out_ref)
    _state_emit(last, sxp_ref, sconv_out_ref, sh_ref, sh_out_ref)


def _rec_ffn(lru_in, ssd_in, x, ya, states, p, bsz, t, ssd_chunk, final_norm, ff_chunk=1024):
    n = x.shape[0]
    nt = n // bsz // t
    keep = CONV_W - 1
    gw = 2 * SSD_HEAD_DIM
    f32 = jnp.float32
    rec_row = lambda width: pl.BlockSpec((t, width), lambda b, i: (b * nt + jnp.minimum(i, nt - 1), 0))
    ffn_row = lambda width: pl.BlockSpec((t, width), lambda b, i: (b * nt + jnp.maximum(i - 1, 0), 0))
    per_b = lambda shape: pl.BlockSpec((1,) + shape, lambda b, i: (b,) + (0,) * len(shape))
    state_specs = [per_b((keep, LRU_WIDTH)), per_b((1, LRU_WIDTH)),
                   per_b((keep, SSD_CONV_CH)), per_b((SSD_GROUPS, gw, SSD_STATE))]
    return pl.pallas_call(
        functools.partial(_rec_ffn_kernel, ff_chunk=ff_chunk, final_norm=final_norm, ssd_chunk=ssd_chunk),
        grid=(bsz, nt + 1),
        in_specs=[rec_row(LRU_IN_W), rec_row(SSD_IN_W), ffn_row(D_MODEL), ffn_row(ATT_WIDTH)] + state_specs
                 + [_resident((CONV_W, LRU_WIDTH)), _resident((1, LRU_WIDTH)),
                    _resident((LRU_WIDTH, 2 * LRU_WIDTH)), _resident((1, 2 * LRU_WIDTH)),
                    _resident((1, LRU_WIDTH)),
                    _resident((CONV_W, SSD_CONV_CH)), _resident((1, SSD_CONV_CH)),
                    _resident((1, DT_PAD)), _resident((1, DT_PAD)),
                    _resident((1, SSD_WIDTH)), _resident((1, SSD_WIDTH)),
                    _resident((D_MODEL, D_MODEL)), _resident((1, D_MODEL)),
                    _resident((D_MODEL, D_FF)), _resident((D_FF, D_MODEL)), _resident((1, D_MODEL))],
        out_specs=[ffn_row(D_MODEL)] + state_specs,
        out_shape=[jax.ShapeDtypeStruct((n, D_MODEL), f32),
                   jax.ShapeDtypeStruct((bsz, keep, LRU_WIDTH), f32),
                   jax.ShapeDtypeStruct((bsz, 1, LRU_WIDTH), f32),
                   jax.ShapeDtypeStruct((bsz, keep, SSD_CONV_CH), f32),
                   jax.ShapeDtypeStruct((bsz, SSD_GROUPS, gw, SSD_STATE), f32)],
        scratch_shapes=[pltpu.VMEM((2, t, LRU_WIDTH + SSD_WIDTH), jnp.bfloat16),
                        pltpu.VMEM((CONV_PAD + t, LRU_WIDTH), f32), pltpu.VMEM((1, LRU_WIDTH), f32),
                        pltpu.VMEM((CONV_PAD + t, SSD_CONV_CH), f32),
                        pltpu.VMEM((SSD_GROUPS, gw, SSD_STATE), f32)],
        compiler_params=_params(_ARB, _ARB),
        name="rec_ffn",
    )(lru_in, ssd_in, x, ya, *states,
      p["lru_conv_w"], p["lru_conv_b"], p["lru_wg"], p["lru_bg"], p["lru_nsp"],
      p["ssd_conv_w"], p["ssd_conv_b"], p["ssd_dt_bias"], p["ssd_a_neg"], p["ssd_d"], p["ssd_norm_g"],
      p["w_out"], p["norm_ffn_g"], p["w_up"], p["w_down"], p["norm_f_g"])


def _out_ffn(x, yl, ys, ya, wo, gf, wu, wd, gn, tm, final_norm, ff_chunk=1024):
    n = x.shape[0]
    row = lambda width: pl.BlockSpec((tm, width), lambda i: (i, 0))
    return pl.pallas_call(
        functools.partial(_out_ffn_kernel, ff_chunk=ff_chunk, final_norm=final_norm),
        grid=(n // tm,),
        in_specs=[row(D_MODEL), row(LRU_WIDTH), row(SSD_WIDTH), row(ATT_WIDTH),
                  _resident((D_MODEL, D_MODEL)), _resident((1, D_MODEL)),
                  _resident((D_MODEL, D_FF)), _resident((D_FF, D_MODEL)), _resident((1, D_MODEL))],
        out_specs=row(D_MODEL),
        out_shape=jax.ShapeDtypeStruct((n, D_MODEL), jnp.float32),
        compiler_params=_params(_ARB),
        name="out_ffn",
    )(x, yl, ys, ya, wo, gf, wu, wd, gn)


def _prep_w_in(w_in):
    o_dt = 2 * LRU_WIDTH + SSD_WIDTH + SSD_CONV_CH
    head = w_in[..., :o_dt]
    dt = jnp.pad(w_in[..., o_dt:o_dt + SSD_HEADS], ((0, 0), (0, 0), (0, DT_PAD - SSD_HEADS)))
    tail = w_in[..., o_dt + SSD_HEADS:]
    return jnp.concatenate([head, dt, tail], axis=-1).astype(jnp.bfloat16)


def _block_diag(w):
    eye = jnp.eye(LRU_BLOCKS, dtype=w.dtype)
    full = w[:, :, :, None, :] * eye[None, :, None, :, None]
    return full.reshape(w.shape[0], LRU_WIDTH, LRU_WIDTH)


def _pad_heads(v):
    return jnp.pad(v, ((0, 0), (0, DT_PAD - SSD_HEADS)))[:, None, :]


def _layer(x, states, kv_all, layer, p, lam_init, bsz, tiles, attn, vt_block, final_norm):
    tm, t_lru, t_ssd, tm_ffn = tiles
    conv_lru0, h_lru0, conv_ssd0, h_ssd0 = states
    lru_in, ssd_in, q, k_all, v_all, kb, vb = _in_proj(x, p["norm_mix_g"], p["w_in"],
                                                       kv_all[0], kv_all[1], layer, tm, vt_block)
    if tm_ffn is None:
        x, conv_lru, h_lru, conv_ssd, h_ssd = _rec_ffn(lru_in, ssd_in, x, attn(q, kb, vb), states, p,
                                                       bsz, t_lru, t_ssd, final_norm)
        return x, (k_all, v_all), (conv_lru, h_lru, conv_ssd, h_ssd)
    y_lru, conv_lru, h_lru = _lru(lru_in, conv_lru0, h_lru0, p["lru_conv_w"], p["lru_conv_b"],
                                  p["lru_wg"], p["lru_bg"], p["lru_nsp"], bsz, t_lru)
    y_ssd, conv_ssd, h_ssd = _ssd(ssd_in, conv_ssd0, h_ssd0, p["ssd_conv_w"], p["ssd_conv_b"],
                                  p["ssd_dt_bias"], p["ssd_a_neg"], p["ssd_d"], p["ssd_norm_g"], bsz, t_ssd)
    y_att = attn(q, kb, vb)
    x = _out_ffn(x, y_lru, y_ssd, y_att, p["w_out"], p["norm_ffn_g"], p["w_up"], p["w_down"],
                 p["norm_f_g"], tm_ffn, final_norm)
    return x, (k_all, v_all), (conv_lru, h_lru, conv_ssd, h_ssd)


def _forward(x_prompt, x_sample, cache_att_k, cache_att_v, state_lru_conv, state_lru_h,
             state_ssd_conv, state_ssd_h, norm_mix_g, w_in, lru_conv_w, lru_conv_b,
             lru_wa, lru_ba, lru_wx, lru_bx, lru_lambda, ssd_conv_w, ssd_conv_b,
             ssd_dt_bias, ssd_a_log, ssd_d, ssd_norm_g, att_lambda, att_subln_g,
             w_out, norm_ffn_g, w_up, w_down, norm_f_g, *, tiles_p, tiles_s, bq):
    f32, bf16 = jnp.float32, jnp.bfloat16
    depth = w_in.shape[0]
    bp, seq, _ = x_prompt.shape
    bs, lq, _ = x_sample.shape
    past_len = cache_att_k.shape[2]
    gw = 2 * SSD_HEAD_DIM

    stacked = {
        "norm_mix_g": norm_mix_g[:, None, :],
        "w_in": _prep_w_in(w_in),
        "lru_conv_w": lru_conv_w, "lru_conv_b": lru_conv_b[:, None, :],
        "lru_wg": jnp.concatenate([_block_diag(lru_wa), _block_diag(lru_wx)], axis=-1).astype(bf16),
        "lru_bg": jnp.concatenate([lru_ba, lru_bx], axis=-1)[:, None, :],
        "lru_nsp": (-LRU_C * jax.nn.softplus(-lru_lambda.astype(f32)))[:, None, :],
        "ssd_conv_w": ssd_conv_w, "ssd_conv_b": ssd_conv_b[:, None, :],
        "ssd_dt_bias": _pad_heads(ssd_dt_bias),
        "ssd_a_neg": _pad_heads(-jnp.exp(ssd_a_log.astype(f32))),
        "ssd_d": jnp.repeat(ssd_d, SSD_HEAD_DIM, axis=-1)[:, None, :],
        "ssd_norm_g": ssd_norm_g[:, None, :],
        "att_lambda": att_lambda, "att_subln_g": att_subln_g[:, None, :],
        "att_subln_g_col": att_subln_g[:, :, None],
        "w_out": w_out.astype(bf16), "norm_ffn_g": norm_ffn_g[:, None, :],
        "w_up": w_up.astype(bf16), "w_down": w_down.astype(bf16),
    }
    gn = norm_f_g[None, :]

    xp = x_prompt.reshape(bp * seq, D_MODEL)
    xs = x_sample.reshape(bs * lq, D_MODEL)
    past_k = cache_att_k.reshape(depth * bs * past_len * ATT_HEADS, ATT_V_DIM)
    past_v = cache_att_v.reshape(depth * bs * past_len * ATT_HEADS, ATT_V_DIM)
    zero_states = (jnp.zeros((bp, CONV_W - 1, LRU_WIDTH), f32), jnp.zeros((bp, 1, LRU_WIDTH), f32),
                   jnp.zeros((bp, CONV_W - 1, SSD_CONV_CH), f32),
                   jnp.zeros((bp, SSD_GROUPS, gw, SSD_STATE), f32))
    kv_p = (jnp.zeros((depth * bp * seq * ATT_HEADS, ATT_V_DIM), f32),) * 2
    kv_s = (jnp.zeros((depth * bs * lq * ATT_HEADS, ATT_V_DIM), f32),) * 2
    st_p, st_s = [], []
    for l in range(depth):
        p = {name: v[l] for name, v in stacked.items()}
        p["norm_f_g"] = gn
        lam_init = 0.8 - 0.6 * math.exp(-0.3 * l)
        last = l == depth - 1
        attn_p = functools.partial(_attn_prompt, lv=p["att_lambda"], g=p["att_subln_g_col"],
                                   lam_init=lam_init, bsz=bp, bq=bq)
        xp, kv_p, sp = _layer(xp, zero_states, kv_p, l, p, lam_init, bp, tiles_p, attn_p, bq, last)
        states_s = (state_lru_conv[l], state_lru_h[l][:, None, :], state_ssd_conv[l],
                    state_ssd_h[l].reshape(bs, SSD_GROUPS, gw, SSD_STATE))
        attn_s = functools.partial(_attn_sample, past_k=past_k, past_v=past_v, layer=l,
                                   lv=p["att_lambda"], g=p["att_subln_g"], lam_init=lam_init,
                                   bsz=bs, lq=lq, past_len=past_len)
        xs, kv_s, ss = _layer(xs, states_s, kv_s, l, p, lam_init, bs, tiles_s, attn_s, None, last)
        st_p.append(sp)
        st_s.append(ss)

    def collect(kv, sts, bsz, length):
        stack = lambda i: jnp.stack([s[i] for s in sts], axis=0)
        return (kv[0].reshape(depth, bsz, length, ATT_HEADS, ATT_V_DIM),
                kv[1].reshape(depth, bsz, length, ATT_HEADS, ATT_V_DIM),
                stack(0), stack(1).reshape(depth, bsz, LRU_WIDTH), stack(2),
                stack(3).reshape(depth, bsz, SSD_HEADS, SSD_HEAD_DIM, SSD_STATE))

    return ((xp.reshape(bp, seq, D_MODEL), xs.reshape(bs, lq, D_MODEL))
            + collect(kv_p, st_p, bp, seq) + collect(kv_s, st_s, bs, lq))


def kernel(x_prompt, x_sample, cache_att_k, cache_att_v, state_lru_conv, state_lru_h, state_ssd_conv, state_ssd_h, norm_mix_g, w_in, lru_conv_w, lru_conv_b, lru_wa, lru_ba, lru_wx, lru_bx, lru_lambda, ssd_conv_w, ssd_conv_b, ssd_dt_bias, ssd_a_log, ssd_d, ssd_norm_g, att_lambda, att_subln_g, w_out, norm_ffn_g, w_up, w_down, norm_f_g):
    lq = x_sample.shape[1]
    n_s = x_sample.shape[0] * lq
    return _forward(x_prompt, x_sample, cache_att_k, cache_att_v, state_lru_conv, state_lru_h,
                    state_ssd_conv, state_ssd_h, norm_mix_g, w_in, lru_conv_w, lru_conv_b,
                    lru_wa, lru_ba, lru_wx, lru_bx, lru_lambda, ssd_conv_w, ssd_conv_b,
                    ssd_dt_bias, ssd_a_log, ssd_d, ssd_norm_g, att_lambda, att_subln_g,
                    w_out, norm_ffn_g, w_up, w_down, norm_f_g,
                    tiles_p=(512, 512, 256, None), tiles_s=(n_s, lq, lq, n_s), bq=1024)
```

```python
import functools
import math

import jax
import jax.numpy as jnp
from jax import lax
from jax.experimental import pallas as pl
from jax.experimental.pallas import tpu as pltpu

D_MODEL = 1024
DEPTH = 4
CHUNK = 64
CONV_W = 4
LRU_WIDTH = 256
LRU_BLOCKS = 4
LRU_BLOCK = LRU_WIDTH // LRU_BLOCKS
LRU_C = 8.0
SSD_WIDTH = 256
SSD_HEAD_DIM = 64
SSD_HEADS = 4
SSD_GROUPS = 2
SSD_STATE = 128
SSD_CONV_CH = SSD_WIDTH + 2 * SSD_GROUPS * SSD_STATE
ATT_WIDTH = 512
ATT_HEADS = 4
ATT_V_DIM = 128
ATT_HEAD_DIM = 64
ATT_SCALE = ATT_HEAD_DIM ** -0.5
LOG2_E = math.log2(math.e)
D_FF = 4 * D_MODEL
EPS = 1e-6

LANES = 128
BF16_SUBLANES = 16
ATT_VT_ROWS = ATT_V_DIM + BF16_SUBLANES
DIAG_STRIP = 256
CONV_PAD = 8
DT_PAD = LANES
LRU_IN_W = 2 * LRU_WIDTH
SSD_IN_W = SSD_WIDTH + SSD_CONV_CH + DT_PAD
IN_COLS_PADDED = LRU_IN_W + SSD_IN_W + 3 * ATT_WIDTH
NEG_BIG = -1e30
VMEM_LIMIT = 56 * 1024 * 1024

_ARB = "arbitrary"


def _params(*sem):
    return pltpu.CompilerParams(dimension_semantics=sem, vmem_limit_bytes=VMEM_LIMIT)


def _resident(shape):
    nd = len(shape)
    return pl.BlockSpec(shape, lambda *_: (0,) * nd, pipeline_mode=pl.Buffered(1))


def _rms(x, g):
    return x * lax.rsqrt(jnp.mean(x * x, axis=-1, keepdims=True) + EPS) * g


def _in_proj_kernel(x_ref, g_ref, w_ref, k_all_ref, v_all_ref,
                    lru_ref, ssd_ref, q_ref, k_ref, v_ref, kb_ref, vb_ref):
    del k_all_ref, v_all_ref
    tm = x_ref.shape[0]
    xn = _rms(x_ref[...], g_ref[...]).astype(jnp.bfloat16)

    def store_heads(ref, val):
        for h in range(ATT_HEADS):
            ref[pl.ds(h, tm, stride=ATT_HEADS), :] = val[:, h * ATT_V_DIM:(h + 1) * ATT_V_DIM]

    def seg(lo, width):
        return jnp.dot(xn, w_ref[:, lo:lo + width], preferred_element_type=jnp.float32)

    lru_ref[...] = seg(0, LRU_IN_W)
    ssd_ref[...] = seg(LRU_IN_W, SSD_IN_W)
    lo = LRU_IN_W + SSD_IN_W
    q_ref[...] = (seg(lo, ATT_WIDTH) * (ATT_SCALE * LOG2_E)).astype(jnp.bfloat16)
    k = seg(lo + ATT_WIDTH, ATT_WIDTH)
    store_heads(k_ref, k)
    kb_ref[...] = k.astype(jnp.bfloat16)
    v = seg(lo + 2 * ATT_WIDTH, ATT_WIDTH)
    store_heads(v_ref, v)
    if len(vb_ref.shape) == 2:
        vb_ref[...] = v.astype(jnp.bfloat16)
    else:
        ones_rows = (lax.broadcasted_iota(jnp.int32, (BF16_SUBLANES, tm), 0) == 0).astype(jnp.bfloat16)
        for h in range(ATT_HEADS):
            base = h * ATT_VT_ROWS
            vb_ref[0, base:base + ATT_V_DIM, :] = v[:, h * ATT_V_DIM:(h + 1) * ATT_V_DIM].T.astype(jnp.bfloat16)
            vb_ref[0, base + ATT_V_DIM:base + ATT_VT_ROWS, :] = ones_rows


def _in_proj(x, g, w, k_all, v_all, layer, tm, bk):
    n = x.shape[0]
    nt = n // tm
    row = lambda width: pl.BlockSpec((tm, width), lambda i: (i, 0))
    heads_rows = pl.BlockSpec((ATT_HEADS * tm, ATT_V_DIM), lambda i: (layer * nt + i, 0))
    untouched = pl.BlockSpec(memory_space=pl.ANY)
    all_shape = jax.ShapeDtypeStruct(k_all.shape, k_all.dtype)
    f32, bf16 = jnp.float32, jnp.bfloat16
    if bk is None:
        vb_spec, vb_shape = row(ATT_WIDTH), jax.ShapeDtypeStruct((n, ATT_WIDTH), bf16)
    else:
        per_block = bk // tm
        assert per_block * tm == bk
        vb_spec = pl.BlockSpec((1, ATT_HEADS * ATT_VT_ROWS, tm), lambda i: (i // per_block, 0, i % per_block))
        vb_shape = jax.ShapeDtypeStruct((n // bk, ATT_HEADS * ATT_VT_ROWS, bk), bf16)
    return pl.pallas_call(
        _in_proj_kernel,
        grid=(nt,),
        in_specs=[row(D_MODEL), _resident((1, D_MODEL)), _resident((D_MODEL, IN_COLS_PADDED)),
                  untouched, untouched],
        out_specs=[row(LRU_IN_W), row(SSD_IN_W), row(ATT_WIDTH), heads_rows, heads_rows,
                   row(ATT_WIDTH), vb_spec],
        out_shape=[jax.ShapeDtypeStruct((n, LRU_IN_W), f32), jax.ShapeDtypeStruct((n, SSD_IN_W), f32),
                   jax.ShapeDtypeStruct((n, ATT_WIDTH), bf16), all_shape, all_shape,
                   jax.ShapeDtypeStruct((n, ATT_WIDTH), bf16), vb_shape],
        input_output_aliases={3: 3, 4: 4},
        compiler_params=_params(_ARB),
        name="in_proj",
    )(x, g, w, k_all, v_all)


def _state_init(first, xp_ref, conv0_ref, h_ref, h0_ref):
    t = xp_ref.shape[0] - CONV_PAD
    keep = CONV_W - 1

    @pl.when(first)
    def _():
        xp_ref[CONV_PAD - keep:CONV_PAD, :] = conv0_ref[0]
        h_ref[...] = h0_ref[0]

    @pl.when(jnp.logical_not(first))
    def _():
        xp_ref[CONV_PAD - keep:CONV_PAD, :] = xp_ref[CONV_PAD + t - keep:CONV_PAD + t, :]


def _state_emit(last, xp_ref, conv_out_ref, h_ref, h_out_ref):
    t = xp_ref.shape[0] - CONV_PAD
    keep = CONV_W - 1

    @pl.when(last)
    def _():
        conv_out_ref[0] = xp_ref[CONV_PAD + t - keep:CONV_PAD + t, :]
        h_out_ref[0] = h_ref[...]


def _causal_conv(xp_ref, x, w_ref, b_ref):
    t = x.shape[0]
    keep = CONV_W - 1
    xp_ref[CONV_PAD:CONV_PAD + t, :] = x
    y = b_ref[...] + w_ref[CONV_W - 1:CONV_W, :] * x
    for j in range(CONV_W - 1):
        y = y + w_ref[j:j + 1, :] * xp_ref[CONV_PAD - keep + j:CONV_PAD - keep + j + t, :]
    return y


def _row_scan(step, vals, t):
    row = lax.broadcasted_iota(jnp.int32, (t, 1), 0)
    d = 1
    while d < t:
        shifted = tuple(pltpu.roll(v, d, 0) for v in vals)
        new = step(shifted, vals)
        vals = tuple(jnp.where(row >= d, n, v) for n, v in zip(new, vals))
        d *= 2
    return vals


def _lru_tile(in_ref, cw_ref, cb_ref, wg_ref, bg_ref, nsp_ref, xp_ref, h_ref):
    t = in_ref.shape[0]
    gate = in_ref[:, 0:LRU_WIDTH]
    x = _causal_conv(xp_ref, in_ref[:, LRU_WIDTH:LRU_IN_W], cw_ref, cb_ref)
    pre = jnp.dot(x.astype(jnp.bfloat16), wg_ref[...], preferred_element_type=jnp.float32) + bg_ref[...]
    r = jax.nn.sigmoid(pre[:, 0:LRU_WIDTH])
    ig = jax.nn.sigmoid(pre[:, LRU_WIDTH:2 * LRU_WIDTH])
    log_a = r * nsp_ref[...]
    a = jnp.exp(log_a)
    b = jnp.sqrt(-jnp.tanh(log_a) * (a * a + 1.0)) * (ig * x)

    def step(shifted, cur):
        a_s, b_s = shifted
        a_c, b_c = cur
        return a_c * a_s, a_c * b_s + b_c

    a_cum, b_cum = _row_scan(step, (a, b), t)
    h = a_cum * h_ref[...] + b_cum
    h_ref[...] = h[t - 1:t, :]
    g = gate
    gelu = 0.5 * g * (1.0 + jnp.tanh(math.sqrt(2.0 / math.pi) * (g + 0.044715 * (g * g * g))))
    return gelu * h


def _lru_kernel(in_ref, conv0_ref, h0_ref, cw_ref, cb_ref, wg_ref, bg_ref, nsp_ref,
                y_ref, conv_out_ref, h_out_ref, xp_ref, h_ref):
    i = pl.program_id(1)
    _state_init(i == 0, xp_ref, conv0_ref, h_ref, h0_ref)
    y = _lru_tile(in_ref, cw_ref, cb_ref, wg_ref, bg_ref, nsp_ref, xp_ref, h_ref)
    y_ref[...] = y.astype(y_ref.dtype)
    _state_emit(i == pl.num_programs(1) - 1, xp_ref, conv_out_ref, h_ref, h_out_ref)


def _lru(lru_in, conv0, h0, cw, cb, wg, bg, nsp, bsz, t):
    n = lru_in.shape[0]
    nt = n // bsz // t
    keep = CONV_W - 1
    per_b = lambda shape: pl.BlockSpec((1,) + shape, lambda b, i: (b,) + (0,) * len(shape))
    f32 = jnp.float32
    return pl.pallas_call(
        _lru_kernel,
        grid=(bsz, nt),
        in_specs=[pl.BlockSpec((t, LRU_IN_W), lambda b, i: (b * nt + i, 0)),
                  per_b((keep, LRU_WIDTH)), per_b((1, LRU_WIDTH)),
                  _resident((CONV_W, LRU_WIDTH)), _resident((1, LRU_WIDTH)),
                  _resident((LRU_WIDTH, 2 * LRU_WIDTH)), _resident((1, 2 * LRU_WIDTH)),
                  _resident((1, LRU_WIDTH))],
        out_specs=[pl.BlockSpec((t, LRU_WIDTH), lambda b, i: (b * nt + i, 0)),
                   per_b((keep, LRU_WIDTH)), per_b((1, LRU_WIDTH))],
        out_shape=[jax.ShapeDtypeStruct((n, LRU_WIDTH), jnp.bfloat16),
                   jax.ShapeDtypeStruct((bsz, keep, LRU_WIDTH), f32),
                   jax.ShapeDtypeStruct((bsz, 1, LRU_WIDTH), f32)],
        scratch_shapes=[pltpu.VMEM((CONV_PAD + t, LRU_WIDTH), f32), pltpu.VMEM((1, LRU_WIDTH), f32)],
        compiler_params=_params(_ARB, _ARB),
        name="lru",
    )(lru_in, conv0, h0, cw, cb, wg, bg, nsp)


def _ssd_tile(in_ref, cw_ref, cb_ref, dtb_ref, aneg_ref, dskip_ref, ng_ref, xp_ref, h_ref, chunk):
    out = None
    for stage in _ssd_stages(in_ref, cw_ref, cb_ref, dtb_ref, aneg_ref, dskip_ref, ng_ref, xp_ref, h_ref, chunk):
        out = stage()
    return out


def _ssd_stages(in_ref, cw_ref, cb_ref, dtb_ref, aneg_ref, dskip_ref, ng_ref, xp_ref, h_ref, chunk):
    t = in_ref.shape[0]
    st = {}

    def prep():
        xc = _causal_conv(xp_ref, in_ref[:, SSD_WIDTH:SSD_WIDTH + SSD_CONV_CH], cw_ref, cb_ref)
        st["xc"] = xc * jax.nn.sigmoid(xc)
        st["dt"] = jax.nn.softplus(in_ref[:, SSD_WIDTH + SSD_CONV_CH:SSD_IN_W] + dtb_ref[...])
        st["h"] = [h_ref[g] for g in range(SSD_GROUPS)]
        st["ys"] = []

    def scan(c):
        rows = slice(c * chunk, (c + 1) * chunk)
        y, st["h"] = _ssd_chunk(in_ref[rows, 0:SSD_WIDTH], st["xc"][rows], st["dt"][rows], st["h"],
                                aneg_ref, dskip_ref, ng_ref)
        st["ys"].append(y)
        if c < t // chunk - 1:
            return None
        for g in range(SSD_GROUPS):
            h_ref[g] = st["h"][g]
        ys = st["ys"]
        return ys[0] if len(ys) == 1 else jnp.concatenate(ys, axis=0)

    return [prep] + [functools.partial(scan, c) for c in range(t // chunk)]


def _ssd_chunk(z, xc, dt, h, aneg_ref, dskip_ref, ng_ref):
    t = z.shape[0]
    bf16 = jnp.bfloat16
    gw = 2 * SSD_HEAD_DIM
    (cs,) = _row_scan(lambda s, c: (s[0] + c[0],), (dt * aneg_ref[...],), t)
    cs_t = cs.T
    cs_last = cs[t - 1:t, :]
    tail = jnp.exp(cs_last - cs)
    grow = jnp.exp(cs)
    chunk_decay = jnp.exp(cs_last)

    lane = lax.broadcasted_iota(jnp.int32, (t, gw), 1)
    low = lane < SSD_HEAD_DIM
    srow = lax.broadcasted_iota(jnp.int32, (gw, 1), 0) < SSD_HEAD_DIM
    causal = lax.broadcasted_iota(jnp.int32, (t, t), 0) >= lax.broadcasted_iota(jnp.int32, (t, t), 1)

    def pick(arr, g):
        c0 = jnp.broadcast_to(arr[:, 2 * g:2 * g + 1], (t, gw))
        c1 = jnp.broadcast_to(arr[:, 2 * g + 1:2 * g + 2], (t, gw))
        return jnp.where(low, c0, c1)

    ys, h_new = [], []
    for g in range(SSD_GROUPS):
        xs = xc[:, g * gw:(g + 1) * gw]
        bm = xc[:, SSD_WIDTH + g * SSD_STATE:SSD_WIDTH + (g + 1) * SSD_STATE].astype(bf16)
        cm = xc[:, SSD_WIDTH + (SSD_GROUPS + g) * SSD_STATE:SSD_WIDTH + (SSD_GROUPS + g + 1) * SSD_STATE].astype(bf16)
        xdt = xs * pick(dt, g)
        xdt_b = xdt.astype(bf16)
        cb = lax.dot_general(cm, bm, (((1,), (1,)), ((), ())), preferred_element_type=jnp.float32)
        y_heads = []
        for hh in range(2):
            hd = 2 * g + hh
            seg = jnp.broadcast_to(cs[:, hd:hd + 1], (t, t)) - cs_t[hd:hd + 1, :]
            scores = (cb * jnp.exp(jnp.where(causal, seg, NEG_BIG))).astype(bf16)
            y_heads.append(jnp.dot(scores, xdt_b, preferred_element_type=jnp.float32))
        y_intra = jnp.where(low, y_heads[0], y_heads[1])
        h_g = h[g]
        y_inter = lax.dot_general(cm, h_g.astype(bf16), (((1,), (1,)), ((), ())),
                                  preferred_element_type=jnp.float32) * pick(grow, g)
        s_local = lax.dot_general((xdt * pick(tail, g)).astype(bf16), bm, (((0,), (0,)), ((), ())),
                                  preferred_element_type=jnp.float32)
        dec = jnp.where(srow, chunk_decay[:, 2 * g:2 * g + 1], chunk_decay[:, 2 * g + 1:2 * g + 2])
        h_new.append(dec * h_g + s_local)
        y = y_intra + y_inter + dskip_ref[:, g * gw:(g + 1) * gw] * xs
        zg = z[:, g * gw:(g + 1) * gw]
        y = y * (zg * jax.nn.sigmoid(zg))
        ys.append(_rms(y, ng_ref[:, g * gw:(g + 1) * gw]))
    return jnp.concatenate(ys, axis=1), h_new


def _ssd_kernel(in_ref, conv0_ref, h0_ref, cw_ref, cb_ref, dtb_ref, aneg_ref, dskip_ref, ng_ref,
                y_ref, conv_out_ref, h_out_ref, xp_ref, h_ref):
    i = pl.program_id(1)
    _state_init(i == 0, xp_ref, conv0_ref, h_ref, h0_ref)
    y = _ssd_tile(in_ref, cw_ref, cb_ref, dtb_ref, aneg_ref, dskip_ref, ng_ref, xp_ref, h_ref,
                  in_ref.shape[0])
    y_ref[...] = y.astype(y_ref.dtype)
    _state_emit(i == pl.num_programs(1) - 1, xp_ref, conv_out_ref, h_ref, h_out_ref)


def _ssd(ssd_in, conv0, h0, cw, cb, dtb, aneg, dskip, ng, bsz, t):
    n = ssd_in.shape[0]
    nt = n // bsz // t
    keep = CONV_W - 1
    gw = 2 * SSD_HEAD_DIM
    per_b = lambda shape: pl.BlockSpec((1,) + shape, lambda b, i: (b,) + (0,) * len(shape))
    f32 = jnp.float32
    return pl.pallas_call(
        _ssd_kernel,
        grid=(bsz, nt),
        in_specs=[pl.BlockSpec((t, SSD_IN_W), lambda b, i: (b * nt + i, 0)),
                  per_b((keep, SSD_CONV_CH)), per_b((SSD_GROUPS, gw, SSD_STATE)),
                  _resident((CONV_W, SSD_CONV_CH)), _resident((1, SSD_CONV_CH)),
                  _resident((1, DT_PAD)), _resident((1, DT_PAD)),
                  _resident((1, SSD_WIDTH)), _resident((1, SSD_WIDTH))],
        out_specs=[pl.BlockSpec((t, SSD_WIDTH), lambda b, i: (b * nt + i, 0)),
                   per_b((keep, SSD_CONV_CH)), per_b((SSD_GROUPS, gw, SSD_STATE))],
        out_shape=[jax.ShapeDtypeStruct((n, SSD_WIDTH), jnp.bfloat16),
                   jax.ShapeDtypeStruct((bsz, keep, SSD_CONV_CH), f32),
                   jax.ShapeDtypeStruct((bsz, SSD_GROUPS, gw, SSD_STATE), f32)],
        scratch_shapes=[pltpu.VMEM((CONV_PAD + t, SSD_CONV_CH), f32),
                        pltpu.VMEM((SSD_GROUPS, gw, SSD_STATE), f32)],
        compiler_params=_params(_ARB, _ARB),
        name="ssd",
    )(ssd_in, conv0, h0, cw, cb, dtb, aneg, dskip, ng)


def _stack_q(q):
    lane = lax.broadcasted_iota(jnp.int32, q.shape, 1)
    zero = jnp.zeros_like(q)
    return jnp.concatenate([jnp.where(lane < ATT_HEAD_DIM, q, zero),
                            jnp.where(lane < ATT_HEAD_DIM, zero, q)], axis=0)


def _diff_lambda(lv, lam_init):
    e1 = jnp.exp(jnp.sum(lv[0:1, :] * lv[1:2, :], axis=1, keepdims=True))
    e2 = jnp.exp(jnp.sum(lv[2:3, :] * lv[3:4, :], axis=1, keepdims=True))
    return e1 - e2 + lam_init


def _attn_finish(acc, l, lam, g, lam_init, bq):
    o = acc[0:bq] / l[0:bq] - lam * (acc[bq:2 * bq] / l[bq:2 * bq])
    return _rms(o, g) * (1.0 - lam_init)


def _attn_prompt_kernel(q_ref, k_ref, vt_ref, lv_ref, g_ref, o_ref, acc0_ref, acc1_ref, s_ref,
                        *, lam_init, bq):
    i = pl.program_id(2)
    accs = (acc0_ref, acc1_ref)
    qs = _stack_q(q_ref[...])
    qs = (qs[0:bq], qs[bq:2 * bq])
    for a in accs:
        a[...] = jnp.zeros(a.shape, jnp.float32)

    def scores(j):
        k = k_ref[pl.ds(pl.multiple_of(j * bq, bq), bq), :]
        return tuple(lax.dot_general(k, qs[mp], (((1,), (1,)), ((), ())),
                                     preferred_element_type=jnp.float32) for mp in range(2))

    def consume(j, slot, stats):
        vt = vt_ref[j]
        out = []
        for mp in range(2):
            m_old = stats[mp]
            s = s_ref[2 * slot + mp]
            m_new = jnp.maximum(m_old, jnp.max(s, axis=0, keepdims=True))
            p = jnp.exp2(s - m_new)
            alpha = jnp.exp2(m_old - m_new)
            accs[mp][...] = alpha * accs[mp][...] + jnp.dot(vt, p.astype(jnp.bfloat16),
                                                            preferred_element_type=jnp.float32)
            out.append(m_new)
        return tuple(out)

    def store_scores(s, slot):
        for mp in range(2):
            s_ref[2 * slot + mp] = s[mp]

    store_scores(scores(0), 0)

    def step(j, slot, stats):
        s_next = scores(j + 1)
        stats = consume(j, slot, stats)
        store_scores(s_next, 1 - slot)
        return stats

    def pair(jj, stats):
        return step(2 * jj + 1, 1, step(2 * jj, 0, stats))

    init = (jnp.full((1, bq), NEG_BIG, jnp.float32),) * 2
    stats = lax.fori_loop(0, lax.shift_right_logical(i, 1), pair, init)

    def consume_diag(slot, stats):
        strip = min(bq, DIAG_STRIP)
        n_strips = bq // strip
        vt = vt_ref[i]
        kpos = lax.broadcasted_iota(jnp.int32, (strip, strip), 0)
        qpos = lax.broadcasted_iota(jnp.int32, (strip, strip), 1)
        square_mask = (kpos // CHUNK) <= (qpos // CHUNK)
        for mp in range(2):
            m_old = stats[mp]
            subs, m_blk = [], None
            for r in range(n_strips):
                sub = s_ref[2 * slot + mp, r * strip:(r + 1) * strip, r * strip:]
                square = jnp.where(square_mask, sub[:, :strip], NEG_BIG)
                sub = square if r == n_strips - 1 else jnp.concatenate([square, sub[:, strip:]], axis=1)
                subs.append(sub)
                mx = jnp.max(sub, axis=0, keepdims=True)
                if r > 0:
                    mx = jnp.concatenate([jnp.full((1, r * strip), NEG_BIG, jnp.float32), mx], axis=1)
                m_blk = mx if m_blk is None else jnp.maximum(m_blk, mx)
            m_new = jnp.maximum(m_old, m_blk)
            alpha = jnp.exp2(m_old - m_new)
            total = None
            for r in range(n_strips):
                p = jnp.exp2(subs[r] - m_new[:, r * strip:]).astype(jnp.bfloat16)
                part = jnp.dot(vt[:, r * strip:(r + 1) * strip], p, preferred_element_type=jnp.float32)
                if r > 0:
                    part = jnp.concatenate([jnp.zeros((ATT_VT_ROWS, r * strip), jnp.float32), part], axis=1)
                total = part if total is None else total + part
            accs[mp][...] = alpha * accs[mp][...] + total

    @pl.when(i % 2 == 0)
    def _():
        consume_diag(0, stats)

    @pl.when(i % 2 == 1)
    def _():
        consume_diag(1, step(i - 1, 0, stats))

    lam = _diff_lambda(lv_ref[...], lam_init)
    d = ATT_V_DIM
    o = (acc0_ref[0:d, :] / acc0_ref[d:d + 1, :]
         - lam * (acc1_ref[0:d, :] / acc1_ref[d:d + 1, :]))
    o = o * lax.rsqrt(jnp.mean(o * o, axis=0, keepdims=True) + EPS) * g_ref[...] * (1.0 - lam_init)
    o_ref[...] = o.T.astype(o_ref.dtype)


def _attn_prompt(q, kb, vt, lv, g, lam_init, bsz, bq):
    n = q.shape[0]
    seq = n // bsz
    nq = seq // bq
    assert bq % CHUNK == 0 and seq % bq == 0 and vt.shape == (n // bq, ATT_HEADS * ATT_VT_ROWS, bq)
    f32 = jnp.float32
    return pl.pallas_call(
        functools.partial(_attn_prompt_kernel, lam_init=lam_init, bq=bq),
        grid=(bsz, ATT_HEADS, nq),
        in_specs=[pl.BlockSpec((bq, ATT_V_DIM), lambda b, h, i: (b * nq + i, h)),
                  pl.BlockSpec((seq, ATT_V_DIM), lambda b, h, i: (b, h)),
                  pl.BlockSpec((nq, ATT_VT_ROWS, bq), lambda b, h, i: (b, h, 0)),
                  pl.BlockSpec((4, ATT_HEAD_DIM), lambda b, h, i: (0, 0)),
                  pl.BlockSpec((ATT_V_DIM, 1), lambda b, h, i: (0, 0))],
        out_specs=pl.BlockSpec((bq, ATT_V_DIM), lambda b, h, i: (b * nq + i, h)),
        out_shape=jax.ShapeDtypeStruct((n, ATT_WIDTH), jnp.bfloat16),
        scratch_shapes=[pltpu.VMEM((ATT_VT_ROWS, bq), f32), pltpu.VMEM((ATT_VT_ROWS, bq), f32),
                        pltpu.VMEM((4, bq, bq), f32)],
        compiler_params=_params(_ARB, _ARB, _ARB),
        name="attn_prompt",
    )(q, kb, vt, lv, g)


def _attn_sample_kernel(q_ref, kn_ref, vn_ref, kp_ref, vp_ref, lv_ref, g_ref, o_ref, *, lam_init, lq):
    lam = _diff_lambda(lv_ref[...], lam_init)
    bf16 = jnp.bfloat16
    nt = (((1,), (1,)), ((), ()))
    past_len = kp_ref.shape[0] // ATT_HEADS
    outs = []
    for h in range(ATT_HEADS):
        sl = slice(h * ATT_V_DIM, (h + 1) * ATT_V_DIM)
        qs = _stack_q(q_ref[:, sl])
        kp = kp_ref[pl.ds(h, past_len, stride=ATT_HEADS), :].astype(bf16)
        vp = vp_ref[pl.ds(h, past_len, stride=ATT_HEADS), :].astype(bf16)
        s_p = lax.dot_general(qs, kp, nt, preferred_element_type=jnp.float32)
        s_n = lax.dot_general(qs, kn_ref[:, sl], nt, preferred_element_type=jnp.float32)
        m = jnp.maximum(jnp.max(s_p, axis=1, keepdims=True), jnp.max(s_n, axis=1, keepdims=True))
        p_p = jnp.exp2(s_p - m)
        p_n = jnp.exp2(s_n - m)
        l = jnp.sum(p_p, axis=1, keepdims=True) + jnp.sum(p_n, axis=1, keepdims=True)
        acc = (jnp.dot(p_p.astype(bf16), vp, preferred_element_type=jnp.float32)
               + jnp.dot(p_n.astype(bf16), vn_ref[:, sl], preferred_element_type=jnp.float32))
        outs.append(_attn_finish(acc, l, lam, g_ref[...], lam_init, lq))
    o_ref[...] = jnp.concatenate(outs, axis=1).astype(o_ref.dtype)


def _attn_sample(q, kb, vb, past_k, past_v, layer, lv, g, lam_init, bsz, lq, past_len):
    assert past_len % CHUNK == 0 and lq <= CHUNK
    n = q.shape[0]
    step = lambda: pl.BlockSpec((lq, ATT_WIDTH), lambda b: (b, 0))
    past = lambda: pl.BlockSpec((past_len * ATT_HEADS, ATT_V_DIM), lambda b: (layer * bsz + b, 0))
    return pl.pallas_call(
        functools.partial(_attn_sample_kernel, lam_init=lam_init, lq=lq),
        grid=(bsz,),
        in_specs=[step(), step(), step(), past(), past(),
                  pl.BlockSpec((4, ATT_HEAD_DIM), lambda b: (0, 0)),
                  pl.BlockSpec((1, ATT_V_DIM), lambda b: (0, 0))],
        out_specs=step(),
        out_shape=jax.ShapeDtypeStruct((n, ATT_WIDTH), jnp.bfloat16),
        compiler_params=_params(_ARB),
        name="attn_sample",
    )(q, kb, vb, past_k, past_v, lv, g)


def _out_ffn_tile(x, mixed, wo_ref, gf_ref, wu_ref, wd_ref, gn_ref, ff_chunk, final_norm):
    out = None
    for stage in _out_ffn_stages(x, mixed, wo_ref, gf_ref, wu_ref, wd_ref, gn_ref, ff_chunk, final_norm):
        out = stage()
    return out


def _out_ffn_stages(x, mixed, wo_ref, gf_ref, wu_ref, wd_ref, gn_ref, ff_chunk, final_norm):
    f32 = jnp.float32
    st = {}
    n_chunks = D_FF // ff_chunk

    def proj():
        st["x"] = x + jnp.dot(mixed, wo_ref[...], preferred_element_type=f32)
        st["hn"] = _rms(st["x"], gf_ref[...]).astype(jnp.bfloat16)

    def mlp(c):
        u = jnp.dot(st["hn"], wu_ref[:, c * ff_chunk:(c + 1) * ff_chunk], preferred_element_type=f32)
        u = jnp.square(jnp.maximum(u, 0.0)).astype(jnp.bfloat16)
        st["x"] = st["x"] + jnp.dot(u, wd_ref[c * ff_chunk:(c + 1) * ff_chunk, :], preferred_element_type=f32)
        if c < n_chunks - 1:
            return None
        return _rms(st["x"], gn_ref[...]) if final_norm else st["x"]

    return [proj] + [functools.partial(mlp, c) for c in range(n_chunks)]


def _out_ffn_kernel(x_ref, yl_ref, ys_ref, ya_ref, wo_ref, gf_ref, wu_ref, wd_ref, gn_ref, o_ref,
                    *, ff_chunk, final_norm):
    mixed = jnp.concatenate([yl_ref[...], ys_ref[...], ya_ref[...]], axis=1)
    o_ref[...] = _out_ffn_tile(x_ref[...], mixed, wo_ref, gf_ref, wu_ref, wd_ref, gn_ref,
                               ff_chunk, final_norm)


def _rec_ffn_kernel(lru_in_ref, ssd_in_ref, x_ref, ya_ref,
                    lconv0_ref, lh0_ref, sconv0_ref, sh0_ref,
                    lcw_ref, lcb_ref, wg_ref, bg_ref, nsp_ref,
                    scw_ref, scb_ref, dtb_ref, aneg_ref, dskip_ref, ng_ref,
                    wo_ref, gf_ref, wu_ref, wd_ref, gn_ref,
                    o_ref, lconv_out_ref, lh_out_ref, sconv_out_ref, sh_out_ref,
                    ymix_ref, lxp_ref, lh_ref, sxp_ref, sh_ref, *, ff_chunk, final_norm, ssd_chunk):
    i = pl.program_id(1)
    tiles = pl.num_programs(1) - 1
    first = i == 0
    _state_init(first, lxp_ref, lconv0_ref, lh_ref, lh0_ref)
    _state_init(first, sxp_ref, sconv0_ref, sh_ref, sh0_ref)

    @pl.when(jnp.logical_and(first, pl.program_id(0) == 0))
    def _():
        ymix_ref[...] = jnp.zeros(ymix_ref.shape, ymix_ref.dtype)

    mixed = jnp.concatenate([ymix_ref[(i + 1) % 2], ya_ref[...]], axis=1)
    ffn = _out_ffn_stages(x_ref[...], mixed, wo_ref, gf_ref, wu_ref, wd_ref, gn_ref, ff_chunk, final_norm)
    rec = ([functools.partial(_lru_tile, lru_in_ref, lcw_ref, lcb_ref, wg_ref, bg_ref, nsp_ref, lxp_ref, lh_ref)]
           + _ssd_stages(ssd_in_ref, scw_ref, scb_ref, dtb_ref, aneg_ref, dskip_ref, ng_ref,
                         sxp_ref, sh_ref, ssd_chunk))
    after = [(r * len(ffn)) // len(rec) for r in range(len(rec))]
    x_new = y_lru = y_ssd = None
    for k in range(len(ffn)):
        x_new = ffn[k]()
        for r in range(len(rec)):
            if after[r] == k:
                y_ssd = rec[r]()
                if r == 0:
                    y_lru = y_ssd
    o_ref[...] = x_new
    ymix_ref[i % 2] = jnp.concatenate([y_lru, y_ssd], axis=1).astype(ymix_ref.dtype)
    last = i == tiles - 1
    _state_emit(last, lxp_ref, lconv_out_ref, lh_ref, lh_out_ref)
    _state_emit(last, sxp_ref, sconv_out_ref, sh_ref, sh_out_ref)


def _rec_ffn(lru_in, ssd_in, x, ya, states, p, bsz, t, ssd_chunk, final_norm, ff_chunk=1024):
    n = x.shape[0]
    nt = n // bsz // t
    keep = CONV_W - 1
    gw = 2 * SSD_HEAD_DIM
    f32 = jnp.float32
    rec_row = lambda width: pl.BlockSpec((t, width), lambda b, i: (b * nt + jnp.minimum(i, nt - 1), 0))
    ffn_row = lambda width: pl.BlockSpec((t, width), lambda b, i: (b * nt + jnp.maximum(i - 1, 0), 0))
    per_b = lambda shape: pl.BlockSpec((1,) + shape, lambda b, i: (b,) + (0,) * len(shape))
    state_specs = [per_b((keep, LRU_WIDTH)), per_b((1, LRU_WIDTH)),
                   per_b((keep, SSD_CONV_CH)), per_b((SSD_GROUPS, gw, SSD_STATE))]
    return pl.pallas_call(
        functools.partial(_rec_ffn_kernel, ff_chunk=ff_chunk, final_norm=final_norm, ssd_chunk=ssd_chunk),
        grid=(bsz, nt + 1),
        in_specs=[rec_row(LRU_IN_W), rec_row(SSD_IN_W), ffn_row(D_MODEL), ffn_row(ATT_WIDTH)] + state_specs
                 + [_resident((CONV_W, LRU_WIDTH)), _resident((1, LRU_WIDTH)),
                    _resident((LRU_WIDTH, 2 * LRU_WIDTH)), _resident((1, 2 * LRU_WIDTH)),
                    _resident((1, LRU_WIDTH)),
                    _resident((CONV_W, SSD_CONV_CH)), _resident((1, SSD_CONV_CH)),
                    _resident((1, DT_PAD)), _resident((1, DT_PAD)),
                    _resident((1, SSD_WIDTH)), _resident((1, SSD_WIDTH)),
                    _resident((D_MODEL, D_MODEL)), _resident((1, D_MODEL)),
                    _resident((D_MODEL, D_FF)), _resident((D_FF, D_MODEL)), _resident((1, D_MODEL))],
        out_specs=[ffn_row(D_MODEL)] + state_specs,
        out_shape=[jax.ShapeDtypeStruct((n, D_MODEL), f32),
                   jax.ShapeDtypeStruct((bsz, keep, LRU_WIDTH), f32),
                   jax.ShapeDtypeStruct((bsz, 1, LRU_WIDTH), f32),
                   jax.ShapeDtypeStruct((bsz, keep, SSD_CONV_CH), f32),
                   jax.ShapeDtypeStruct((bsz, SSD_GROUPS, gw, SSD_STATE), f32)],
        scratch_shapes=[pltpu.VMEM((2, t, LRU_WIDTH + SSD_WIDTH), jnp.bfloat16),
                        pltpu.VMEM((CONV_PAD + t, LRU_WIDTH), f32), pltpu.VMEM((1, LRU_WIDTH), f32),
                        pltpu.VMEM((CONV_PAD + t, SSD_CONV_CH), f32),
                        pltpu.VMEM((SSD_GROUPS, gw, SSD_STATE), f32)],
        compiler_params=_params(_ARB, _ARB),
        name="rec_ffn",
    )(lru_in, ssd_in, x, ya, *states,
      p["lru_conv_w"], p["lru_conv_b"], p["lru_wg"], p["lru_bg"], p["lru_nsp"],
      p["ssd_conv_w"], p["ssd_conv_b"], p["ssd_dt_bias"], p["ssd_a_neg"], p["ssd_d"], p["ssd_norm_g"],
      p["w_out"], p["norm_ffn_g"], p["w_up"], p["w_down"], p["norm_f_g"])


def _out_ffn(x, yl, ys, ya, wo, gf, wu, wd, gn, tm, final_norm, ff_chunk=1024):
    n = x.shape[0]
    row = lambda width: pl.BlockSpec((tm, width), lambda i: (i, 0))
    return pl.pallas_call(
        functools.partial(_out_ffn_kernel, ff_chunk=ff_chunk, final_norm=final_norm),
        grid=(n // tm,),
        in_specs=[row(D_MODEL), row(LRU_WIDTH), row(SSD_WIDTH), row(ATT_WIDTH),
                  _resident((D_MODEL, D_MODEL)), _resident((1, D_MODEL)),
                  _resident((D_MODEL, D_FF)), _resident((D_FF, D_MODEL)), _resident((1, D_MODEL))],
        out_specs=row(D_MODEL),
        out_shape=jax.ShapeDtypeStruct((n, D_MODEL), jnp.float32),
        compiler_params=_params(_ARB),
        name="out_ffn",
    )(x, yl, ys, ya, wo, gf, wu, wd, gn)


def _prep_w_in(w_in):
    o_dt = 2 * LRU_WIDTH + SSD_WIDTH + SSD_CONV_CH
    head = w_in[..., :o_dt]
    dt = jnp.pad(w_in[..., o_dt:o_dt + SSD_HEADS], ((0, 0), (0, 0), (0, DT_PAD - SSD_HEADS)))
    tail = w_in[..., o_dt + SSD_HEADS:]
    return jnp.concatenate([head, dt, tail], axis=-1).astype(jnp.bfloat16)


def _block_diag(w):
    eye = jnp.eye(LRU_BLOCKS, dtype=w.dtype)
    full = w[:, :, :, None, :] * eye[None, :, None, :, None]
    return full.reshape(w.shape[0], LRU_WIDTH, LRU_WIDTH)


def _pad_heads(v):
    return jnp.pad(v, ((0, 0), (0, DT_PAD - SSD_HEADS)))[:, None, :]


def _layer(x, states, kv_all, layer, p, lam_init, bsz, tiles, attn, vt_block, final_norm):
    tm, t_lru, t_ssd, tm_ffn = tiles
    conv_lru0, h_lru0, conv_ssd0, h_ssd0 = states
    lru_in, ssd_in, q, k_all, v_all, kb, vb = _in_proj(x, p["norm_mix_g"], p["w_in"],
                                                       kv_all[0], kv_all[1], layer, tm, vt_block)
    if tm_ffn is None:
        x, conv_lru, h_lru, conv_ssd, h_ssd = _rec_ffn(lru_in, ssd_in, x, attn(q, kb, vb), states, p,
                                                       bsz, t_lru, t_ssd, final_norm)
        return x, (k_all, v_all), (conv_lru, h_lru, conv_ssd, h_ssd)
    y_lru, conv_lru, h_lru = _lru(lru_in, conv_lru0, h_lru0, p["lru_conv_w"], p["lru_conv_b"],
                                  p["lru_wg"], p["lru_bg"], p["lru_nsp"], bsz, t_lru)
    y_ssd, conv_ssd, h_ssd = _ssd(ssd_in, conv_ssd0, h_ssd0, p["ssd_conv_w"], p["ssd_conv_b"],
                                  p["ssd_dt_bias"], p["ssd_a_neg"], p["ssd_d"], p["ssd_norm_g"], bsz, t_ssd)
    y_att = attn(q, kb, vb)
    x = _out_ffn(x, y_lru, y_ssd, y_att, p["w_out"], p["norm_ffn_g"], p["w_up"], p["w_down"],
                 p["norm_f_g"], tm_ffn, final_norm)
    return x, (k_all, v_all), (conv_lru, h_lru, conv_ssd, h_ssd)


def _forward(x_prompt, x_sample, cache_att_k, cache_att_v, state_lru_conv, state_lru_h,
             state_ssd_conv, state_ssd_h, norm_mix_g, w_in, lru_conv_w, lru_conv_b,
             lru_wa, lru_ba, lru_wx, lru_bx, lru_lambda, ssd_conv_w, ssd_conv_b,
             ssd_dt_bias, ssd_a_log, ssd_d, ssd_norm_g, att_lambda, att_subln_g,
             w_out, norm_ffn_g, w_up, w_down, norm_f_g, *, tiles_p, tiles_s, bq):
    f32, bf16 = jnp.float32, jnp.bfloat16
    depth = w_in.shape[0]
    bp, seq, _ = x_prompt.shape
    bs, lq, _ = x_sample.shape
    past_len = cache_att_k.shape[2]
    gw = 2 * SSD_HEAD_DIM

    stacked = {
        "norm_mix_g": norm_mix_g[:, None, :],
        "w_in": _prep_w_in(w_in),
        "lru_conv_w": lru_conv_w, "lru_conv_b": lru_conv_b[:, None, :],
        "lru_wg": jnp.concatenate([_block_diag(lru_wa), _block_diag(lru_wx)], axis=-1).astype(bf16),
        "lru_bg": jnp.concatenate([lru_ba, lru_bx], axis=-1)[:, None, :],
        "lru_nsp": (-LRU_C * jax.nn.softplus(-lru_lambda.astype(f32)))[:, None, :],
        "ssd_conv_w": ssd_conv_w, "ssd_conv_b": ssd_conv_b[:, None, :],
        "ssd_dt_bias": _pad_heads(ssd_dt_bias),
        "ssd_a_neg": _pad_heads(-jnp.exp(ssd_a_log.astype(f32))),
        "ssd_d": jnp.repeat(ssd_d, SSD_HEAD_DIM, axis=-1)[:, None, :],
        "ssd_norm_g": ssd_norm_g[:, None, :],
        "att_lambda": att_lambda, "att_subln_g": att_subln_g[:, None, :],
        "att_subln_g_col": att_subln_g[:, :, None],
        "w_out": w_out.astype(bf16), "norm_ffn_g": norm_ffn_g[:, None, :],
        "w_up": w_up.astype(bf16), "w_down": w_down.astype(bf16),
    }
    gn = norm_f_g[None, :]

    xp = x_prompt.reshape(bp * seq, D_MODEL)
    xs = x_sample.reshape(bs * lq, D_MODEL)
    past_k = cache_att_k.reshape(depth * bs * past_len * ATT_HEADS, ATT_V_DIM)
    past_v = cache_att_v.reshape(depth * bs * past_len * ATT_HEADS, ATT_V_DIM)
    zero_states = (jnp.zeros((bp, CONV_W - 1, LRU_WIDTH), f32), jnp.zeros((bp, 1, LRU_WIDTH), f32),
                   jnp.zeros((bp, CONV_W - 1, SSD_CONV_CH), f32),
                   jnp.zeros((bp, SSD_GROUPS, gw, SSD_STATE), f32))
    kv_p = (jnp.zeros((depth * bp * seq * ATT_HEADS, ATT_V_DIM), f32),) * 2
    kv_s = (jnp.zeros((depth * bs * lq * ATT_HEADS, ATT_V_DIM), f32),) * 2
    st_p, st_s = [], []
    for l in range(depth):
        p = {name: v[l] for name, v in stacked.items()}
        p["norm_f_g"] = gn
        lam_init = 0.8 - 0.6 * math.exp(-0.3 * l)
        last = l == depth - 1
        attn_p = functools.partial(_attn_prompt, lv=p["att_lambda"], g=p["att_subln_g_col"],
                                   lam_init=lam_init, bsz=bp, bq=bq)
        xp, kv_p, sp = _layer(xp, zero_states, kv_p, l, p, lam_init, bp, tiles_p, attn_p, bq, last)
        states_s = (state_lru_conv[l], state_lru_h[l][:, None, :], state_ssd_conv[l],
                    state_ssd_h[l].reshape(bs, SSD_GROUPS, gw, SSD_STATE))
        attn_s = functools.partial(_attn_sample, past_k=past_k, past_v=past_v, layer=l,
                                   lv=p["att_lambda"], g=p["att_subln_g"], lam_init=lam_init,
                                   bsz=bs, lq=lq, past_len=past_len)
        xs, kv_s, ss = _layer(xs, states_s, kv_s, l, p, lam_init, bs, tiles_s, attn_s, None, last)
        st_p.append(sp)
        st_s.append(ss)

    def collect(kv, sts, bsz, length):
        stack = lambda i: jnp.stack([s[i] for s in sts], axis=0)
        return (kv[0].reshape(depth, bsz, length, ATT_HEADS, ATT_V_DIM),
                kv[1].reshape(depth, bsz, length, ATT_HEADS, ATT_V_DIM),
                stack(0), stack(1).reshape(depth, bsz, LRU_WIDTH), stack(2),
                stack(3).reshape(depth, bsz, SSD_HEADS, SSD_HEAD_DIM, SSD_STATE))

    return ((xp.reshape(bp, seq, D_MODEL), xs.reshape(bs, lq, D_MODEL))
            + collect(kv_p, st_p, bp, seq) + collect(kv_s, st_s, bs, lq))


def kernel(x_prompt, x_sample, cache_att_k, cache_att_v, state_lru_conv, state_lru_h, state_ssd_conv, state_ssd_h, norm_mix_g, w_in, lru_conv_w, lru_conv_b, lru_wa, lru_ba, lru_wx, lru_bx, lru_lambda, ssd_conv_w, ssd_conv_b, ssd_dt_bias, ssd_a_log, ssd_d, ssd_norm_g, att_lambda, att_subln_g, w_out, norm_ffn_g, w_up, w_down, norm_f_g):
    lq = x_sample.shape[1]
    n_s = x_sample.shape[0] * lq
    return _forward(x_prompt, x_sample, cache_att_k, cache_att_v, state_lru_conv, state_lru_h,
                    state_ssd_conv, state_ssd_h, norm_mix_g, w_in, lru_conv_w, lru_conv_b,
                    lru_wa, lru_ba, lru_wx, lru_bx, lru_lambda, ssd_conv_w, ssd_conv_b,
                    ssd_dt_bias, ssd_a_log, ssd_d, ssd_norm_g, att_lambda, att_subln_g,
                    w_out, norm_ffn_g, w_up, w_down, norm_f_g,
                    tiles_p=(512, 512, 256, None), tiles_s=(n_s, lq, lq, n_s), bq=1024)
```

```python
import functools
import math

import jax
import jax.numpy as jnp
from jax import lax
from jax.experimental import pallas as pl
from jax.experimental.pallas import tpu as pltpu

D_MODEL = 1024
DEPTH = 4
CHUNK = 64
CONV_W = 4
LRU_WIDTH = 256
LRU_BLOCKS = 4
LRU_BLOCK = LRU_WIDTH // LRU_BLOCKS
LRU_C = 8.0
SSD_WIDTH = 256
SSD_HEAD_DIM = 64
SSD_HEADS = 4
SSD_GROUPS = 2
SSD_STATE = 128
SSD_CONV_CH = SSD_WIDTH + 2 * SSD_GROUPS * SSD_STATE
ATT_WIDTH = 512
ATT_HEADS = 4
ATT_V_DIM = 128
ATT_HEAD_DIM = 64
ATT_SCALE = ATT_HEAD_DIM ** -0.5
LOG2_E = math.log2(math.e)
D_FF = 4 * D_MODEL
EPS = 1e-6

LANES = 128
BF16_SUBLANES = 16
ATT_VT_ROWS = ATT_V_DIM + BF16_SUBLANES
DIAG_STRIP = 256
CONV_PAD = 8
DT_PAD = LANES
LRU_IN_W = 2 * LRU_WIDTH
SSD_IN_W = SSD_WIDTH + SSD_CONV_CH + DT_PAD
IN_COLS_PADDED = LRU_IN_W + SSD_IN_W + 3 * ATT_WIDTH
NEG_BIG = -1e30
VMEM_LIMIT = 56 * 1024 * 1024

_ARB = "arbitrary"


def _params(*sem):
    return pltpu.CompilerParams(dimension_semantics=sem, vmem_limit_bytes=VMEM_LIMIT)


def _resident(shape):
    nd = len(shape)
    return pl.BlockSpec(shape, lambda *_: (0,) * nd, pipeline_mode=pl.Buffered(1))


def _rms(x, g):
    return x * lax.rsqrt(jnp.mean(x * x, axis=-1, keepdims=True) + EPS) * g


def _in_proj_kernel(x_ref, g_ref, w_ref, k_all_ref, v_all_ref,
                    lru_ref, ssd_ref, q_ref, k_ref, v_ref, kb_ref, vb_ref):
    del k_all_ref, v_all_ref
    tm = x_ref.shape[0]
    xn = _rms(x_ref[...], g_ref[...]).astype(jnp.bfloat16)

    def store_heads(ref, val):
        for h in range(ATT_HEADS):
            ref[pl.ds(h, tm, stride=ATT_HEADS), :] = val[:, h * ATT_V_DIM:(h + 1) * ATT_V_DIM]

    def seg(lo, width):
        return jnp.dot(xn, w_ref[:, lo:lo + width], preferred_element_type=jnp.float32)

    lru_ref[...] = seg(0, LRU_IN_W)
    ssd_ref[...] = seg(LRU_IN_W, SSD_IN_W)
    lo = LRU_IN_W + SSD_IN_W
    q_ref[...] = (seg(lo, ATT_WIDTH) * (ATT_SCALE * LOG2_E)).astype(jnp.bfloat16)
    k = seg(lo + ATT_WIDTH, ATT_WIDTH)
    store_heads(k_ref, k)
    kb_ref[...] = k.astype(jnp.bfloat16)
    v = seg(lo + 2 * ATT_WIDTH, ATT_WIDTH)
    store_heads(v_ref, v)
    if len(vb_ref.shape) == 2:
        vb_ref[...] = v.astype(jnp.bfloat16)
    else:
        ones_rows = (lax.broadcasted_iota(jnp.int32, (BF16_SUBLANES, tm), 0) == 0).astype(jnp.bfloat16)
        for h in range(ATT_HEADS):
            base = h * ATT_VT_ROWS
            vb_ref[0, base:base + ATT_V_DIM, :] = v[:, h * ATT_V_DIM:(h + 1) * ATT_V_DIM].T.astype(jnp.bfloat16)
            vb_ref[0, base + ATT_V_DIM:base + ATT_VT_ROWS, :] = ones_rows


def _in_proj(x, g, w, k_all, v_all, layer, tm, bk):
    n = x.shape[0]
    nt = n // tm
    row = lambda width: pl.BlockSpec((tm, width), lambda i: (i, 0))
    heads_rows = pl.BlockSpec((ATT_HEADS * tm, ATT_V_DIM), lambda i: (layer * nt + i, 0))
    untouched = pl.BlockSpec(memory_space=pl.ANY)
    all_shape = jax.ShapeDtypeStruct(k_all.shape, k_all.dtype)
    f32, bf16 = jnp.float32, jnp.bfloat16
    if bk is None:
        vb_spec, vb_shape = row(ATT_WIDTH), jax.ShapeDtypeStruct((n, ATT_WIDTH), bf16)
    else:
        per_block = bk // tm
        assert per_block * tm == bk
        vb_spec = pl.BlockSpec((1, ATT_HEADS * ATT_VT_ROWS, tm), lambda i: (i // per_block, 0, i % per_block))
        vb_shape = jax.ShapeDtypeStruct((n // bk, ATT_HEADS * ATT_VT_ROWS, bk), bf16)
    return pl.pallas_call(
        _in_proj_kernel,
        grid=(nt,),
        in_specs=[row(D_MODEL), _resident((1, D_MODEL)), _resident((D_MODEL, IN_COLS_PADDED)),
                  untouched, untouched],
        out_specs=[row(LRU_IN_W), row(SSD_IN_W), row(ATT_WIDTH), heads_rows, heads_rows,
                   row(ATT_WIDTH), vb_spec],
        out_shape=[jax.ShapeDtypeStruct((n, LRU_IN_W), f32), jax.ShapeDtypeStruct((n, SSD_IN_W), f32),
                   jax.ShapeDtypeStruct((n, ATT_WIDTH), bf16), all_shape, all_shape,
                   jax.ShapeDtypeStruct((n, ATT_WIDTH), bf16), vb_shape],
        input_output_aliases={3: 3, 4: 4},
        compiler_params=_params(_ARB),
        name="in_proj",
    )(x, g, w, k_all, v_all)


def _state_init(first, xp_ref, conv0_ref, h_ref, h0_ref):
    t = xp_ref.shape[0] - CONV_PAD
    keep = CONV_W - 1

    @pl.when(first)
    def _():
        xp_ref[CONV_PAD - keep:CONV_PAD, :] = conv0_ref[0]
        h_ref[...] = h0_ref[0]

    @pl.when(jnp.logical_not(first))
    def _():
        xp_ref[CONV_PAD - keep:CONV_PAD, :] = xp_ref[CONV_PAD + t - keep:CONV_PAD + t, :]


def _state_emit(last, xp_ref, conv_out_ref, h_ref, h_out_ref):
    t = xp_ref.shape[0] - CONV_PAD
    keep = CONV_W - 1

    @pl.when(last)
    def _():
        conv_out_ref[0] = xp_ref[CONV_PAD + t - keep:CONV_PAD + t, :]
        h_out_ref[0] = h_ref[...]


def _causal_conv(xp_ref, x, w_ref, b_ref):
    t = x.shape[0]
    keep = CONV_W - 1
    xp_ref[CONV_PAD:CONV_PAD + t, :] = x
    y = b_ref[...] + w_ref[CONV_W - 1:CONV_W, :] * x
    for j in range(CONV_W - 1):
        y = y + w_ref[j:j + 1, :] * xp_ref[CONV_PAD - keep + j:CONV_PAD - keep + j + t, :]
    return y


def _row_scan(step, vals, t):
    row = lax.broadcasted_iota(jnp.int32, (t, 1), 0)
    d = 1
    while d < t:
        shifted = tuple(pltpu.roll(v, d, 0) for v in vals)
        new = step(shifted, vals)
        vals = tuple(jnp.where(row >= d, n, v) for n, v in zip(new, vals))
        d *= 2
    return vals


def _lru_tile(in_ref, cw_ref, cb_ref, wg_ref, bg_ref, nsp_ref, xp_ref, h_ref):
    t = in_ref.shape[0]
    gate = in_ref[:, 0:LRU_WIDTH]
    x = _causal_conv(xp_ref, in_ref[:, LRU_WIDTH:LRU_IN_W], cw_ref, cb_ref)
    pre = jnp.dot(x.astype(jnp.bfloat16), wg_ref[...], preferred_element_type=jnp.float32) + bg_ref[...]
    r = jax.nn.sigmoid(pre[:, 0:LRU_WIDTH])
    ig = jax.nn.sigmoid(pre[:, LRU_WIDTH:2 * LRU_WIDTH])
    log_a = r * nsp_ref[...]
    a = jnp.exp(log_a)
    b = jnp.sqrt(-jnp.tanh(log_a) * (a * a + 1.0)) * (ig * x)

    def step(shifted, cur):
        a_s, b_s = shifted
        a_c, b_c = cur
        return a_c * a_s, a_c * b_s + b_c

    a_cum, b_cum = _row_scan(step, (a, b), t)
    h = a_cum * h_ref[...] + b_cum
    h_ref[...] = h[t - 1:t, :]
    g = gate
    gelu = 0.5 * g * (1.0 + jnp.tanh(math.sqrt(2.0 / math.pi) * (g + 0.044715 * (g * g * g))))
    return gelu * h


def _lru_kernel(in_ref, conv0_ref, h0_ref, cw_ref, cb_ref, wg_ref, bg_ref, nsp_ref,
                y_ref, conv_out_ref, h_out_ref, xp_ref, h_ref):
    i = pl.program_id(1)
    _state_init(i == 0, xp_ref, conv0_ref, h_ref, h0_ref)
    y = _lru_tile(in_ref, cw_ref, cb_ref, wg_ref, bg_ref, nsp_ref, xp_ref, h_ref)
    y_ref[...] = y.astype(y_ref.dtype)
    _state_emit(i == pl.num_programs(1) - 1, xp_ref, conv_out_ref, h_ref, h_out_ref)


def _lru(lru_in, conv0, h0, cw, cb, wg, bg, nsp, bsz, t):
    n = lru_in.shape[0]
    nt = n // bsz // t
    keep = CONV_W - 1
    per_b = lambda shape: pl.BlockSpec((1,) + shape, lambda b, i: (b,) + (0,) * len(shape))
    f32 = jnp.float32
    return pl.pallas_call(
        _lru_kernel,
        grid=(bsz, nt),
        in_specs=[pl.BlockSpec((t, LRU_IN_W), lambda b, i: (b * nt + i, 0)),
                  per_b((keep, LRU_WIDTH)), per_b((1, LRU_WIDTH)),
                  _resident((CONV_W, LRU_WIDTH)), _resident((1, LRU_WIDTH)),
                  _resident((LRU_WIDTH, 2 * LRU_WIDTH)), _resident((1, 2 * LRU_WIDTH)),
                  _resident((1, LRU_WIDTH))],
        out_specs=[pl.BlockSpec((t, LRU_WIDTH), lambda b, i: (b * nt + i, 0)),
                   per_b((keep, LRU_WIDTH)), per_b((1, LRU_WIDTH))],
        out_shape=[jax.ShapeDtypeStruct((n, LRU_WIDTH), jnp.bfloat16),
                   jax.ShapeDtypeStruct((bsz, keep, LRU_WIDTH), f32),
                   jax.ShapeDtypeStruct((bsz, 1, LRU_WIDTH), f32)],
        scratch_shapes=[pltpu.VMEM((CONV_PAD + t, LRU_WIDTH), f32), pltpu.VMEM((1, LRU_WIDTH), f32)],
        compiler_params=_params(_ARB, _ARB),
        name="lru",
    )(lru_in, conv0, h0, cw, cb, wg, bg, nsp)


def _ssd_tile(in_ref, cw_ref, cb_ref, dtb_ref, aneg_ref, dskip_ref, ng_ref, xp_ref, h_ref, chunk):
    out = None
    for stage in _ssd_stages(in_ref, cw_ref, cb_ref, dtb_ref, aneg_ref, dskip_ref, ng_ref, xp_ref, h_ref, chunk):
        out = stage()
    return out


def _ssd_stages(in_ref, cw_ref, cb_ref, dtb_ref, aneg_ref, dskip_ref, ng_ref, xp_ref, h_ref, chunk):
    t = in_ref.shape[0]
    st = {}

    def prep():
        xc = _causal_conv(xp_ref, in_ref[:, SSD_WIDTH:SSD_WIDTH + SSD_CONV_CH], cw_ref, cb_ref)
        st["xc"] = xc * jax.nn.sigmoid(xc)
        st["dt"] = jax.nn.softplus(in_ref[:, SSD_WIDTH + SSD_CONV_CH:SSD_IN_W] + dtb_ref[...])
        st["h"] = [h_ref[g] for g in range(SSD_GROUPS)]
        st["ys"] = []

    def scan(c):
        rows = slice(c * chunk, (c + 1) * chunk)
        y, st["h"] = _ssd_chunk(in_ref[rows, 0:SSD_WIDTH], st["xc"][rows], st["dt"][rows], st["h"],
                                aneg_ref, dskip_ref, ng_ref)
        st["ys"].append(y)
        if c < t // chunk - 1:
            return None
        for g in range(SSD_GROUPS):
            h_ref[g] = st["h"][g]
        ys = st["ys"]
        return ys[0] if len(ys) == 1 else jnp.concatenate(ys, axis=0)

    return [prep] + [functools.partial(scan, c) for c in range(t // chunk)]


def _ssd_chunk(z, xc, dt, h, aneg_ref, dskip_ref, ng_ref):
    t = z.shape[0]
    bf16 = jnp.bfloat16
    gw = 2 * SSD_HEAD_DIM
    (cs,) = _row_scan(lambda s, c: (s[0] + c[0],), (dt * aneg_ref[...],), t)
    cs_t = cs.T
    cs_last = cs[t - 1:t, :]
    tail = jnp.exp(cs_last - cs)
    grow = jnp.exp(cs)
    chunk_decay = jnp.exp(cs_last)

    lane = lax.broadcasted_iota(jnp.int32, (t, gw), 1)
    low = lane < SSD_HEAD_DIM
    srow = lax.broadcasted_iota(jnp.int32, (gw, 1), 0) < SSD_HEAD_DIM
    causal = lax.broadcasted_iota(jnp.int32, (t, t), 0) >= lax.broadcasted_iota(jnp.int32, (t, t), 1)

    def pick(arr, g):
        c0 = jnp.broadcast_to(arr[:, 2 * g:2 * g + 1], (t, gw))
        c1 = jnp.broadcast_to(arr[:, 2 * g + 1:2 * g + 2], (t, gw))
        return jnp.where(low, c0, c1)

    ys, h_new = [], []
    for g in range(SSD_GROUPS):
        xs = xc[:, g * gw:(g + 1) * gw]
        bm = xc[:, SSD_WIDTH + g * SSD_STATE:SSD_WIDTH + (g + 1) * SSD_STATE].astype(bf16)
        cm = xc[:, SSD_WIDTH + (SSD_GROUPS + g) * SSD_STATE:SSD_WIDTH + (SSD_GROUPS + g + 1) * SSD_STATE].astype(bf16)
        xdt = xs * pick(dt, g)
        xdt_b = xdt.astype(bf16)
        cb = lax.dot_general(cm, bm, (((1,), (1,)), ((), ())), preferred_element_type=jnp.float32)
        y_heads = []
        for hh in range(2):
            hd = 2 * g + hh
            seg = jnp.broadcast_to(cs[:, hd:hd + 1], (t, t)) - cs_t[hd:hd + 1, :]
            scores = (cb * jnp.exp(jnp.where(causal, seg, NEG_BIG))).astype(bf16)
            y_heads.append(jnp.dot(scores, xdt_b, preferred_element_type=jnp.float32))
        y_intra = jnp.where(low, y_heads[0], y_heads[1])
        h_g = h[g]
        y_inter = lax.dot_general(cm, h_g.astype(bf16), (((1,), (1,)), ((), ())),
                                  preferred_element_type=jnp.float32) * pick(grow, g)
        s_local = lax.dot_general((xdt * pick(tail, g)).astype(bf16), bm, (((0,), (0,)), ((), ())),
                                  preferred_element_type=jnp.float32)
        dec = jnp.where(srow, chunk_decay[:, 2 * g:2 * g + 1], chunk_decay[:, 2 * g + 1:2 * g + 2])
        h_new.append(dec * h_g + s_local)
        y = y_intra + y_inter + dskip_ref[:, g * gw:(g + 1) * gw] * xs
        zg = z[:, g * gw:(g + 1) * gw]
        y = y * (zg * jax.nn.sigmoid(zg))
        ys.append(_rms(y, ng_ref[:, g * gw:(g + 1) * gw]))
    return jnp.concatenate(ys, axis=1), h_new


def _ssd_kernel(in_ref, conv0_ref, h0_ref, cw_ref, cb_ref, dtb_ref, aneg_ref, dskip_ref, ng_ref,
                y_ref, conv_out_ref, h_out_ref, xp_ref, h_ref):
    i = pl.program_id(1)
    _state_init(i == 0, xp_ref, conv0_ref, h_ref, h0_ref)
    y = _ssd_tile(in_ref, cw_ref, cb_ref, dtb_ref, aneg_ref, dskip_ref, ng_ref, xp_ref, h_ref,
                  in_ref.shape[0])
    y_ref[...] = y.astype(y_ref.dtype)
    _state_emit(i == pl.num_programs(1) - 1, xp_ref, conv_out_ref, h_ref, h_out_ref)


def _ssd(ssd_in, conv0, h0, cw, cb, dtb, aneg, dskip, ng, bsz, t):
    n = ssd_in.shape[0]
    nt = n // bsz // t
    keep = CONV_W - 1
    gw = 2 * SSD_HEAD_DIM
    per_b = lambda shape: pl.BlockSpec((1,) + shape, lambda b, i: (b,) + (0,) * len(shape))
    f32 = jnp.float32
    return pl.pallas_call(
        _ssd_kernel,
        grid=(bsz, nt),
        in_specs=[pl.BlockSpec((t, SSD_IN_W), lambda b, i: (b * nt + i, 0)),
                  per_b((keep, SSD_CONV_CH)), per_b((SSD_GROUPS, gw, SSD_STATE)),
                  _resident((CONV_W, SSD_CONV_CH)), _resident((1, SSD_CONV_CH)),
                  _resident((1, DT_PAD)), _resident((1, DT_PAD)),
                  _resident((1, SSD_WIDTH)), _resident((1, SSD_WIDTH))],
        out_specs=[pl.BlockSpec((t, SSD_WIDTH), lambda b, i: (b * nt + i, 0)),
                   per_b((keep, SSD_CONV_CH)), per_b((SSD_GROUPS, gw, SSD_STATE))],
        out_shape=[jax.ShapeDtypeStruct((n, SSD_WIDTH), jnp.bfloat16),
                   jax.ShapeDtypeStruct((bsz, keep, SSD_CONV_CH), f32),
                   jax.ShapeDtypeStruct((bsz, SSD_GROUPS, gw, SSD_STATE), f32)],
        scratch_shapes=[pltpu.VMEM((CONV_PAD + t, SSD_CONV_CH), f32),
                        pltpu.VMEM((SSD_GROUPS, gw, SSD_STATE), f32)],
        compiler_params=_params(_ARB, _ARB),
        name="ssd",
    )(ssd_in, conv0, h0, cw, cb, dtb, aneg, dskip, ng)


def _stack_q(q):
    lane = lax.broadcasted_iota(jnp.int32, q.shape, 1)
    zero = jnp.zeros_like(q)
    return jnp.concatenate([jnp.where(lane < ATT_HEAD_DIM, q, zero),
                            jnp.where(lane < ATT_HEAD_DIM, zero, q)], axis=0)


def _diff_lambda(lv, lam_init):
    e1 = jnp.exp(jnp.sum(lv[0:1, :] * lv[1:2, :], axis=1, keepdims=True))
    e2 = jnp.exp(jnp.sum(lv[2:3, :] * lv[3:4, :], axis=1, keepdims=True))
    return e1 - e2 + lam_init


def _attn_finish(acc, l, lam, g, lam_init, bq):
    o = acc[0:bq] / l[0:bq] - lam * (acc[bq:2 * bq] / l[bq:2 * bq])
    return _rms(o, g) * (1.0 - lam_init)


def _attn_prompt_kernel(q_ref, qn_ref, k_ref, vt_ref, lv_ref, g_ref, o_ref, acc0_ref, acc1_ref, s_ref,
                        *, lam_init, bq):
    i = pl.program_id(2)
    accs = (acc0_ref, acc1_ref)
    qs = _stack_q(q_ref[...])
    qs = (qs[0:bq], qs[bq:2 * bq])
    for a in accs:
        a[...] = jnp.zeros(a.shape, jnp.float32)

    def scores(j, queries=qs):
        k = k_ref[pl.ds(pl.multiple_of(j * bq, bq), bq), :]
        return tuple(lax.dot_general(k, queries[mp], (((1,), (1,)), ((), ())),
                                     preferred_element_type=jnp.float32) for mp in range(2))

    def next_first_scores():
        qn = _stack_q(qn_ref[...])
        return scores(0, (qn[0:bq], qn[bq:2 * bq]))

    def consume(j, slot, stats):
        vt = vt_ref[j]
        out = []
        for mp in range(2):
            m_old = stats[mp]
            s = s_ref[2 * slot + mp]
            m_new = jnp.maximum(m_old, jnp.max(s, axis=0, keepdims=True))
            p = jnp.exp2(s - m_new)
            alpha = jnp.exp2(m_old - m_new)
            accs[mp][...] = alpha * accs[mp][...] + jnp.dot(vt, p.astype(jnp.bfloat16),
                                                            preferred_element_type=jnp.float32)
            out.append(m_new)
        return tuple(out)

    def store_scores(s, slot):
        for mp in range(2):
            s_ref[2 * slot + mp] = s[mp]

    @pl.when(i == 0)
    def _():
        store_scores(scores(0), 0)

    def step(j, slot, stats):
        s_next = scores(j + 1)
        stats = consume(j, slot, stats)
        store_scores(s_next, 1 - slot)
        return stats

    def pair(jj, stats):
        return step(2 * jj + 1, 1, step(2 * jj, 0, stats))

    init = (jnp.full((1, bq), NEG_BIG, jnp.float32),) * 2
    stats = lax.fori_loop(0, lax.shift_right_logical(i, 1), pair, init)

    def consume_diag(slot, stats):
        strip = min(bq, DIAG_STRIP)
        n_strips = bq // strip
        vt = vt_ref[i]
        kpos = lax.broadcasted_iota(jnp.int32, (strip, strip), 0)
        qpos = lax.broadcasted_iota(jnp.int32, (strip, strip), 1)
        square_mask = (kpos // CHUNK) <= (qpos // CHUNK)
        for mp in range(2):
            m_old = stats[mp]
            subs, m_blk = [], None
            for r in range(n_strips):
                sub = s_ref[2 * slot + mp, r * strip:(r + 1) * strip, r * strip:]
                square = jnp.where(square_mask, sub[:, :strip], NEG_BIG)
                sub = square if r == n_strips - 1 else jnp.concatenate([square, sub[:, strip:]], axis=1)
                subs.append(sub)
                mx = jnp.max(sub, axis=0, keepdims=True)
                if r > 0:
                    mx = jnp.concatenate([jnp.full((1, r * strip), NEG_BIG, jnp.float32), mx], axis=1)
                m_blk = mx if m_blk is None else jnp.maximum(m_blk, mx)
            m_new = jnp.maximum(m_old, m_blk)
            alpha = jnp.exp2(m_old - m_new)
            total = None
            for r in range(n_strips):
                p = jnp.exp2(subs[r] - m_new[:, r * strip:]).astype(jnp.bfloat16)
                part = jnp.dot(vt[:, r * strip:(r + 1) * strip], p, preferred_element_type=jnp.float32)
                if r > 0:
                    part = jnp.concatenate([jnp.zeros((ATT_VT_ROWS, r * strip), jnp.float32), part], axis=1)
                total = part if total is None else total + part
            accs[mp][...] = alpha * accs[mp][...] + total

    @pl.when(i % 2 == 0)
    def _():
        s_next = next_first_scores()
        consume_diag(0, stats)
        store_scores(s_next, 0)

    @pl.when(i % 2 == 1)
    def _():
        st = step(i - 1, 0, stats)
        s_next = next_first_scores()
        consume_diag(1, st)
        store_scores(s_next, 0)

    lam = _diff_lambda(lv_ref[...], lam_init)
    d = ATT_V_DIM
    o = (acc0_ref[0:d, :] / acc0_ref[d:d + 1, :]
         - lam * (acc1_ref[0:d, :] / acc1_ref[d:d + 1, :]))
    o = o * lax.rsqrt(jnp.mean(o * o, axis=0, keepdims=True) + EPS) * g_ref[...] * (1.0 - lam_init)
    o_ref[...] = o.T.astype(o_ref.dtype)


def _attn_prompt(q, kb, vt, lv, g, lam_init, bsz, bq):
    n = q.shape[0]
    seq = n // bsz
    nq = seq // bq
    assert bq % CHUNK == 0 and seq % bq == 0 and vt.shape == (n // bq, ATT_HEADS * ATT_VT_ROWS, bq)
    f32 = jnp.float32
    return pl.pallas_call(
        functools.partial(_attn_prompt_kernel, lam_init=lam_init, bq=bq),
        grid=(bsz, ATT_HEADS, nq),
        in_specs=[pl.BlockSpec((bq, ATT_V_DIM), lambda b, h, i: (b * nq + i, h)),
                  pl.BlockSpec((bq, ATT_V_DIM), lambda b, h, i: (b * nq + jnp.minimum(i + 1, nq - 1), h)),
                  pl.BlockSpec((seq, ATT_V_DIM), lambda b, h, i: (b, h)),
                  pl.BlockSpec((nq, ATT_VT_ROWS, bq), lambda b, h, i: (b, h, 0)),
                  pl.BlockSpec((4, ATT_HEAD_DIM), lambda b, h, i: (0, 0)),
                  pl.BlockSpec((ATT_V_DIM, 1), lambda b, h, i: (0, 0))],
        out_specs=pl.BlockSpec((bq, ATT_V_DIM), lambda b, h, i: (b * nq + i, h)),
        out_shape=jax.ShapeDtypeStruct((n, ATT_WIDTH), jnp.bfloat16),
        scratch_shapes=[pltpu.VMEM((ATT_VT_ROWS, bq), f32), pltpu.VMEM((ATT_VT_ROWS, bq), f32),
                        pltpu.VMEM((4, bq, bq), f32)],
        compiler_params=_params(_ARB, _ARB, _ARB),
        name="attn_prompt",
    )(q, q, kb, vt, lv, g)


def _attn_sample_kernel(q_ref, kn_ref, vn_ref, kp_ref, vp_ref, lv_ref, g_ref, o_ref, *, lam_init, lq):
    lam = _diff_lambda(lv_ref[...], lam_init)
    bf16 = jnp.bfloat16
    nt = (((1,), (1,)), ((), ()))
    past_len = kp_ref.shape[0] // ATT_HEADS
    outs = []
    for h in range(ATT_HEADS):
        sl = slice(h * ATT_V_DIM, (h + 1) * ATT_V_DIM)
        qs = _stack_q(q_ref[:, sl])
        kp = kp_ref[pl.ds(h, past_len, stride=ATT_HEADS), :].astype(bf16)
        vp = vp_ref[pl.ds(h, past_len, stride=ATT_HEADS), :].astype(bf16)
        s_p = lax.dot_general(qs, kp, nt, preferred_element_type=jnp.float32)
        s_n = lax.dot_general(qs, kn_ref[:, sl], nt, preferred_element_type=jnp.float32)
        m = jnp.maximum(jnp.max(s_p, axis=1, keepdims=True), jnp.max(s_n, axis=1, keepdims=True))
        p_p = jnp.exp2(s_p - m)
        p_n = jnp.exp2(s_n - m)
        l = jnp.sum(p_p, axis=1, keepdims=True) + jnp.sum(p_n, axis=1, keepdims=True)
        acc = (jnp.dot(p_p.astype(bf16), vp, preferred_element_type=jnp.float32)
               + jnp.dot(p_n.astype(bf16), vn_ref[:, sl], preferred_element_type=jnp.float32))
        outs.append(_attn_finish(acc, l, lam, g_ref[...], lam_init, lq))
    o_ref[...] = jnp.concatenate(outs, axis=1).astype(o_ref.dtype)


def _attn_sample(q, kb, vb, past_k, past_v, layer, lv, g, lam_init, bsz, lq, past_len):
    assert past_len % CHUNK == 0 and lq <= CHUNK
    n = q.shape[0]
    step = lambda: pl.BlockSpec((lq, ATT_WIDTH), lambda b: (b, 0))
    past = lambda: pl.BlockSpec((past_len * ATT_HEADS, ATT_V_DIM), lambda b: (layer * bsz + b, 0))
    return pl.pallas_call(
        functools.partial(_attn_sample_kernel, lam_init=lam_init, lq=lq),
        grid=(bsz,),
        in_specs=[step(), step(), step(), past(), past(),
                  pl.BlockSpec((4, ATT_HEAD_DIM), lambda b: (0, 0)),
                  pl.BlockSpec((1, ATT_V_DIM), lambda b: (0, 0))],
        out_specs=step(),
        out_shape=jax.ShapeDtypeStruct((n, ATT_WIDTH), jnp.bfloat16),
        compiler_params=_params(_ARB),
        name="attn_sample",
    )(q, kb, vb, past_k, past_v, lv, g)


def _out_ffn_tile(x, mixed, wo_ref, gf_ref, wu_ref, wd_ref, gn_ref, ff_chunk, final_norm):
    out = None
    for stage in _out_ffn_stages(x, mixed, wo_ref, gf_ref, wu_ref, wd_ref, gn_ref, ff_chunk, final_norm):
        out = stage()
    return out


def _out_ffn_stages(x, mixed, wo_ref, gf_ref, wu_ref, wd_ref, gn_ref, ff_chunk, final_norm):
    f32 = jnp.float32
    st = {}
    n_chunks = D_FF // ff_chunk

    def proj():
        st["x"] = x + jnp.dot(mixed, wo_ref[...], preferred_element_type=f32)
        st["hn"] = _rms(st["x"], gf_ref[...]).astype(jnp.bfloat16)

    def mlp(c):
        u = jnp.dot(st["hn"], wu_ref[:, c * ff_chunk:(c + 1) * ff_chunk], preferred_element_type=f32)
        u = jnp.square(jnp.maximum(u, 0.0)).astype(jnp.bfloat16)
        st["x"] = st["x"] + jnp.dot(u, wd_ref[c * ff_chunk:(c + 1) * ff_chunk, :], preferred_element_type=f32)
        if c < n_chunks - 1:
            return None
        return _rms(st["x"], gn_ref[...]) if final_norm else st["x"]

    return [proj] + [functools.partial(mlp, c) for c in range(n_chunks)]


def _out_ffn_kernel(x_ref, yl_ref, ys_ref, ya_ref, wo_ref, gf_ref, wu_ref, wd_ref, gn_ref, o_ref,
                    *, ff_chunk, final_norm):
    mixed = jnp.concatenate([yl_ref[...], ys_ref[...], ya_ref[...]], axis=1)
    o_ref[...] = _out_ffn_tile(x_ref[...], mixed, wo_ref, gf_ref, wu_ref, wd_ref, gn_ref,
                               ff_chunk, final_norm)


def _rec_ffn_kernel(lru_in_ref, ssd_in_ref, x_ref, ya_ref,
                    lconv0_ref, lh0_ref, sconv0_ref, sh0_ref,
                    lcw_ref, lcb_ref, wg_ref, bg_ref, nsp_ref,
                    scw_ref, scb_ref, dtb_ref, aneg_ref, dskip_ref, ng_ref,
                    wo_ref, gf_ref, wu_ref, wd_ref, gn_ref,
                    o_ref, lconv_out_ref, lh_out_ref, sconv_out_ref, sh_out_ref,
                    ymix_ref, lxp_ref, lh_ref, sxp_ref, sh_ref, *, ff_chunk, final_norm, ssd_chunk):
    i = pl.program_id(1)
    tiles = pl.num_programs(1) - 1
    first = i == 0
    _state_init(first, lxp_ref, lconv0_ref, lh_ref, lh0_ref)
    _state_init(first, sxp_ref, sconv0_ref, sh_ref, sh0_ref)

    @pl.when(jnp.logical_and(first, pl.program_id(0) == 0))
    def _():
        ymix_ref[...] = jnp.zeros(ymix_ref.shape, ymix_ref.dtype)

    mixed = jnp.concatenate([ymix_ref[(i + 1) % 2], ya_ref[...]], axis=1)
    ffn = _out_ffn_stages(x_ref[...], mixed, wo_ref, gf_ref, wu_ref, wd_ref, gn_ref, ff_chunk, final_norm)
    rec = ([functools.partial(_lru_tile, lru_in_ref, lcw_ref, lcb_ref, wg_ref, bg_ref, nsp_ref, lxp_ref, lh_ref)]
           + _ssd_stages(ssd_in_ref, scw_ref, scb_ref, dtb_ref, aneg_ref, dskip_ref, ng_ref,
                         sxp_ref, sh_ref, ssd_chunk))
    after = [(r * len(ffn)) // len(rec) for r in range(len(rec))]
    x_new = y_lru = y_ssd = None
    for k in range(len(ffn)):
        x_new = ffn[k]()
        for r in range(len(rec)):
            if after[r] == k:
                y_ssd = rec[r]()
                if r == 0:
                    y_lru = y_ssd
    o_ref[...] = x_new
    ymix_ref[i % 2] = jnp.concatenate([y_lru, y_ssd], axis=1).astype(ymix_ref.dtype)
    last = i == tiles - 1
    _state_emit(last, lxp_ref, lconv_out_ref, lh_ref, lh_out_ref)
    _state_emit(last, sxp_ref, sconv_out_ref, sh_ref, sh_out_ref)


def _rec_ffn(lru_in, ssd_in, x, ya, states, p, bsz, t, ssd_chunk, final_norm, ff_chunk=1024):
    n = x.shape[0]
    nt = n // bsz // t
    keep = CONV_W - 1
    gw = 2 * SSD_HEAD_DIM
    f32 = jnp.float32
    rec_row = lambda width: pl.BlockSpec((t, width), lambda b, i: (b * nt + jnp.minimum(i, nt - 1), 0))
    ffn_row = lambda width: pl.BlockSpec((t, width), lambda b, i: (b * nt + jnp.maximum(i - 1, 0), 0))
    per_b = lambda shape: pl.BlockSpec((1,) + shape, lambda b, i: (b,) + (0,) * len(shape))
    state_specs = [per_b((keep, LRU_WIDTH)), per_b((1, LRU_WIDTH)),
                   per_b((keep, SSD_CONV_CH)), per_b((SSD_GROUPS, gw, SSD_STATE))]
    return pl.pallas_call(
        functools.partial(_rec_ffn_kernel, ff_chunk=ff_chunk, final_norm=final_norm, ssd_chunk=ssd_chunk),
        grid=(bsz, nt + 1),
        in_specs=[rec_row(LRU_IN_W), rec_row(SSD_IN_W), ffn_row(D_MODEL), ffn_row(ATT_WIDTH)] + state_specs
                 + [_resident((CONV_W, LRU_WIDTH)), _resident((1, LRU_WIDTH)),
                    _resident((LRU_WIDTH, 2 * LRU_WIDTH)), _resident((1, 2 * LRU_WIDTH)),
                    _resident((1, LRU_WIDTH)),
                    _resident((CONV_W, SSD_CONV_CH)), _resident((1, SSD_CONV_CH)),
                    _resident((1, DT_PAD)), _resident((1, DT_PAD)),
                    _resident((1, SSD_WIDTH)), _resident((1, SSD_WIDTH)),
                    _resident((D_MODEL, D_MODEL)), _resident((1, D_MODEL)),
                    _resident((D_MODEL, D_FF)), _resident((D_FF, D_MODEL)), _resident((1, D_MODEL))],
        out_specs=[ffn_row(D_MODEL)] + state_specs,
        out_shape=[jax.ShapeDtypeStruct((n, D_MODEL), f32),
                   jax.ShapeDtypeStruct((bsz, keep, LRU_WIDTH), f32),
                   jax.ShapeDtypeStruct((bsz, 1, LRU_WIDTH), f32),
                   jax.ShapeDtypeStruct((bsz, keep, SSD_CONV_CH), f32),
                   jax.ShapeDtypeStruct((bsz, SSD_GROUPS, gw, SSD_STATE), f32)],
        scratch_shapes=[pltpu.VMEM((2, t, LRU_WIDTH + SSD_WIDTH), jnp.bfloat16),
                        pltpu.VMEM((CONV_PAD + t, LRU_WIDTH), f32), pltpu.VMEM((1, LRU_WIDTH), f32),
                        pltpu.VMEM((CONV_PAD + t, SSD_CONV_CH), f32),
                        pltpu.VMEM((SSD_GROUPS, gw, SSD_STATE), f32)],
        compiler_params=_params(_ARB, _ARB),
        name="rec_ffn",
    )(lru_in, ssd_in, x, ya, *states,
      p["lru_conv_w"], p["lru_conv_b"], p["lru_wg"], p["lru_bg"], p["lru_nsp"],
      p["ssd_conv_w"], p["ssd_conv_b"], p["ssd_dt_bias"], p["ssd_a_neg"], p["ssd_d"], p["ssd_norm_g"],
      p["w_out"], p["norm_ffn_g"], p["w_up"], p["w_down"], p["norm_f_g"])


def _out_ffn(x, yl, ys, ya, wo, gf, wu, wd, gn, tm, final_norm, ff_chunk=1024):
    n = x.shape[0]
    row = lambda width: pl.BlockSpec((tm, width), lambda i: (i, 0))
    return pl.pallas_call(
        functools.partial(_out_ffn_kernel, ff_chunk=ff_chunk, final_norm=final_norm),
        grid=(n // tm,),
        in_specs=[row(D_MODEL), row(LRU_WIDTH), row(SSD_WIDTH), row(ATT_WIDTH),
                  _resident((D_MODEL, D_MODEL)), _resident((1, D_MODEL)),
                  _resident((D_MODEL, D_FF)), _resident((D_FF, D_MODEL)), _resident((1, D_MODEL))],
        out_specs=row(D_MODEL),
        out_shape=jax.ShapeDtypeStruct((n, D_MODEL), jnp.float32),
        compiler_params=_params(_ARB),
        name="out_ffn",
    )(x, yl, ys, ya, wo, gf, wu, wd, gn)


def _prep_w_in(w_in):
    o_dt = 2 * LRU_WIDTH + SSD_WIDTH + SSD_CONV_CH
    head = w_in[..., :o_dt]
    dt = jnp.pad(w_in[..., o_dt:o_dt + SSD_HEADS], ((0, 0), (0, 0), (0, DT_PAD - SSD_HEADS)))
    tail = w_in[..., o_dt + SSD_HEADS:]
    return jnp.concatenate([head, dt, tail], axis=-1).astype(jnp.bfloat16)


def _block_diag(w):
    eye = jnp.eye(LRU_BLOCKS, dtype=w.dtype)
    full = w[:, :, :, None, :] * eye[None, :, None, :, None]
    return full.reshape(w.shape[0], LRU_WIDTH, LRU_WIDTH)


def _pad_heads(v):
    return jnp.pad(v, ((0, 0), (0, DT_PAD - SSD_HEADS)))[:, None, :]


def _layer(x, states, kv_all, layer, p, lam_init, bsz, tiles, attn, vt_block, final_norm):
    tm, t_lru, t_ssd, tm_ffn = tiles
    conv_lru0, h_lru0, conv_ssd0, h_ssd0 = states
    lru_in, ssd_in, q, k_all, v_all, kb, vb = _in_proj(x, p["norm_mix_g"], p["w_in"],
                                                       kv_all[0], kv_all[1], layer, tm, vt_block)
    if tm_ffn is None:
        x, conv_lru, h_lru, conv_ssd, h_ssd = _rec_ffn(lru_in, ssd_in, x, attn(q, kb, vb), states, p,
                                                       bsz, t_lru, t_ssd, final_norm)
        return x, (k_all, v_all), (conv_lru, h_lru, conv_ssd, h_ssd)
    y_lru, conv_lru, h_lru = _lru(lru_in, conv_lru0, h_lru0, p["lru_conv_w"], p["lru_conv_b"],
                                  p["lru_wg"], p["lru_bg"], p["lru_nsp"], bsz, t_lru)
    y_ssd, conv_ssd, h_ssd = _ssd(ssd_in, conv_ssd0, h_ssd0, p["ssd_conv_w"], p["ssd_conv_b"],
                                  p["ssd_dt_bias"], p["ssd_a_neg"], p["ssd_d"], p["ssd_norm_g"], bsz, t_ssd)
    y_att = attn(q, kb, vb)
    x = _out_ffn(x, y_lru, y_ssd, y_att, p["w_out"], p["norm_ffn_g"], p["w_up"], p["w_down"],
                 p["norm_f_g"], tm_ffn, final_norm)
    return x, (k_all, v_all), (conv_lru, h_lru, conv_ssd, h_ssd)


def _forward(x_prompt, x_sample, cache_att_k, cache_att_v, state_lru_conv, state_lru_h,
             state_ssd_conv, state_ssd_h, norm_mix_g, w_in, lru_conv_w, lru_conv_b,
             lru_wa, lru_ba, lru_wx, lru_bx, lru_lambda, ssd_conv_w, ssd_conv_b,
             ssd_dt_bias, ssd_a_log, ssd_d, ssd_norm_g, att_lambda, att_subln_g,
             w_out, norm_ffn_g, w_up, w_down, norm_f_g, *, tiles_p, tiles_s, bq):
    f32, bf16 = jnp.float32, jnp.bfloat16
    depth = w_in.shape[0]
    bp, seq, _ = x_prompt.shape
    bs, lq, _ = x_sample.shape
    past_len = cache_att_k.shape[2]
    gw = 2 * SSD_HEAD_DIM

    stacked = {
        "norm_mix_g": norm_mix_g[:, None, :],
        "w_in": _prep_w_in(w_in),
        "lru_conv_w": lru_conv_w, "lru_conv_b": lru_conv_b[:, None, :],
        "lru_wg": jnp.concatenate([_block_diag(lru_wa), _block_diag(lru_wx)], axis=-1).astype(bf16),
        "lru_bg": jnp.concatenate([lru_ba, lru_bx], axis=-1)[:, None, :],
        "lru_nsp": (-LRU_C * jax.nn.softplus(-lru_lambda.astype(f32)))[:, None, :],
        "ssd_conv_w": ssd_conv_w, "ssd_conv_b": ssd_conv_b[:, None, :],
        "ssd_dt_bias": _pad_heads(ssd_dt_bias),
        "ssd_a_neg": _pad_heads(-jnp.exp(ssd_a_log.astype(f32))),
        "ssd_d": jnp.repeat(ssd_d, SSD_HEAD_DIM, axis=-1)[:, None, :],
        "ssd_norm_g": ssd_norm_g[:, None, :],
        "att_lambda": att_lambda, "att_subln_g": att_subln_g[:, None, :],
        "att_subln_g_col": att_subln_g[:, :, None],
        "w_out": w_out.astype(bf16), "norm_ffn_g": norm_ffn_g[:, None, :],
        "w_up": w_up.astype(bf16), "w_down": w_down.astype(bf16),
    }
    gn = norm_f_g[None, :]

    xp = x_prompt.reshape(bp * seq, D_MODEL)
    xs = x_sample.reshape(bs * lq, D_MODEL)
    past_k = cache_att_k.reshape(depth * bs * past_len * ATT_HEADS, ATT_V_DIM)
    past_v = cache_att_v.reshape(depth * bs * past_len * ATT_HEADS, ATT_V_DIM)
    zero_states = (jnp.zeros((bp, CONV_W - 1, LRU_WIDTH), f32), jnp.zeros((bp, 1, LRU_WIDTH), f32),
                   jnp.zeros((bp, CONV_W - 1, SSD_CONV_CH), f32),
                   jnp.zeros((bp, SSD_GROUPS, gw, SSD_STATE), f32))
    kv_p = (jnp.zeros((depth * bp * seq * ATT_HEADS, ATT_V_DIM), f32),) * 2
    kv_s = (jnp.zeros((depth * bs * lq * ATT_HEADS, ATT_V_DIM), f32),) * 2
    st_p, st_s = [], []
    for l in range(depth):
        p = {name: v[l] for name, v in stacked.items()}
        p["norm_f_g"] = gn
        lam_init = 0.8 - 0.6 * math.exp(-0.3 * l)
        last = l == depth - 1
        attn_p = functools.partial(_attn_prompt, lv=p["att_lambda"], g=p["att_subln_g_col"],
                                   lam_init=lam_init, bsz=bp, bq=bq)
        xp, kv_p, sp = _layer(xp, zero_states, kv_p, l, p, lam_init, bp, tiles_p, attn_p, bq, last)
        states_s = (state_lru_conv[l], state_lru_h[l][:, None, :], state_ssd_conv[l],
                    state_ssd_h[l].reshape(bs, SSD_GROUPS, gw, SSD_STATE))
        attn_s = functools.partial(_attn_sample, past_k=past_k, past_v=past_v, layer=l,
                                   lv=p["att_lambda"], g=p["att_subln_g"], lam_init=lam_init,
                                   bsz=bs, lq=lq, past_len=past_len)
        xs, kv_s, ss = _layer(xs, states_s, kv_s, l, p, lam_init, bs, tiles_s, attn_s, None, last)
        st_p.append(sp)
        st_s.append(ss)

    def collect(kv, sts, bsz, length):
        stack = lambda i: jnp.stack([s[i] for s in sts], axis=0)
        return (kv[0].reshape(depth, bsz, length, ATT_HEADS, ATT_V_DIM),
                kv[1].reshape(depth, bsz, length, ATT_HEADS, ATT_V_DIM),
                stack(0), stack(1).reshape(depth, bsz, LRU_WIDTH), stack(2),
                stack(3).reshape(depth, bsz, SSD_HEADS, SSD_HEAD_DIM, SSD_STATE))

    return ((xp.reshape(bp, seq, D_MODEL), xs.reshape(bs, lq, D_MODEL))
            + collect(kv_p, st_p, bp, seq) + collect(kv_s, st_s, bs, lq))


def kernel(x_prompt, x_sample, cache_att_k, cache_att_v, state_lru_conv, state_lru_h, state_ssd_conv, state_ssd_h, norm_mix_g, w_in, lru_conv_w, lru_conv_b, lru_wa, lru_ba, lru_wx, lru_bx, lru_lambda, ssd_conv_w, ssd_conv_b, ssd_dt_bias, ssd_a_log, ssd_d, ssd_norm_g, att_lambda, att_subln_g, w_out, norm_ffn_g, w_up, w_down, norm_f_g):
    lq = x_sample.shape[1]
    n_s = x_sample.shape[0] * lq
    return _forward(x_prompt, x_sample, cache_att_k, cache_att_v, state_lru_conv, state_lru_h,
                    state_ssd_conv, state_ssd_h, norm_mix_g, w_in, lru_conv_w, lru_conv_b,
                    lru_wa, lru_ba, lru_wx, lru_bx, lru_lambda, ssd_conv_w, ssd_conv_b,
                    ssd_dt_bias, ssd_a_log, ssd_d, ssd_norm_g, att_lambda, att_subln_g,
                    w_out, norm_ffn_g, w_up, w_down, norm_f_g,
                    tiles_p=(512, 512, 256, None), tiles_s=(n_s, lq, lq, n_s), bq=1024)
```
